```python
import jax
import jax.numpy as jnp
from jax import lax
import numpy as np

D_MODEL = 4096
BATCH = 2
SEQ = 8192
DEPTH = 2
DEC_BATCH = 8
DEC_SEQ = 64
PAST_LEN = 4096

CHUNK = 64
N_A_LAYERS = DEPTH // 2
N_B_LAYERS = DEPTH - N_A_LAYERS
A_CHUNK = 128
A_WIDTH = D_MODEL
A_GROUPS = 16
A_GROUP_DIM = A_WIDTH // A_GROUPS
HEAD_DIM = 128
N_HEADS = D_MODEL // HEAD_DIM
N_KV_HEADS = 8
KV_REP = N_HEADS // N_KV_HEADS
PAST_CHUNKS = 8
BAND_LEN = (PAST_CHUNKS + 1) * CHUNK
MAX_REL = 128
N_REL = 2 * MAX_REL + 1
N_EXPERTS = 16
N_EXPERT_GROUPS = 4
EXPERTS_PER_GROUP = N_EXPERTS // N_EXPERT_GROUPS
TOP_K = 2
EXPERT_FF = D_MODEL // 4
EPS = 1e-6
NEG_INF = -1e30

kernel_name = 'yoco_gmlp_chunkband_moe_stream_step'


def _rmsnorm(x, g):
    xf = x.astype(jnp.float32)
    y = xf * lax.rsqrt(jnp.mean(xf * xf, axis=-1, keepdims=True) + EPS)
    return (y * g.astype(jnp.float32)).astype(x.dtype)


def _layernorm(x, g, b):
    xf = x.astype(jnp.float32)
    xc = xf - jnp.mean(xf, axis=-1, keepdims=True)
    y = xc * lax.rsqrt(jnp.mean(xc * xc, axis=-1, keepdims=True) + EPS)
    return (y * g.astype(jnp.float32) + b.astype(jnp.float32)).astype(x.dtype)


def _ada(c, w, b, n):
    m = jax.nn.silu(c) @ w + b
    return [t[:, None, :] for t in jnp.split(m, n, axis=-1)]


def _modulate(xn, shift, scale):
    return xn * (1.0 + scale) + shift


def _chunk_mlp_mix(h, w_in, b_in, ln_g, ln_b, w_s, b_s, w_out):
    bn, L, _ = h.shape
    z = jax.nn.gelu(h @ w_in + b_in)
    u, v = jnp.split(z, 2, axis=-1)
    v = _layernorm(v, ln_g, ln_b)
    pad = (-L) % A_CHUNK
    vp = jnp.pad(v, ((0, 0), (0, pad), (0, 0)))
    nc = (L + pad) // A_CHUNK
    vc = vp.reshape(bn, nc, A_CHUNK, A_GROUPS, A_GROUP_DIM)
    causal = jnp.tril(jnp.ones((A_CHUNK, A_CHUNK), dtype=bool))
    wm = jnp.where(causal, w_s, 0)
    s = jnp.einsum('gts,bnsgc->bntgc', wm, vc) + b_s.T[None, None, :, :, None]
    s = s.reshape(bn, nc * A_CHUNK, A_WIDTH)[:, :L]
    return (u * s) @ w_out, v


def _band_attention(q, k, v, q_pos, k_pos, rel_bias):
    bn, lq = q.shape[:2]
    lk = k.shape[1]
    qg = q.reshape(bn, lq, N_KV_HEADS, KV_REP, HEAD_DIM)
    s = jnp.einsum('bqgrd,bkgd->bgrqk', qg, k).astype(jnp.float32) * (HEAD_DIM ** -0.5)
    rel = jnp.clip(q_pos[:, None] - k_pos[None, :], -MAX_REL, MAX_REL) + MAX_REL
    bias = rel_bias[:, rel].astype(jnp.float32).reshape(N_KV_HEADS, KV_REP, lq, lk)
    qc = q_pos // CHUNK
    kc = k_pos // CHUNK
    visible = ((k_pos[None, :] >= 0) & (kc[None, :] <= qc[:, None])
               & (kc[None, :] >= qc[:, None] - PAST_CHUNKS))
    s = jnp.where(visible, s + bias, NEG_INF)
    p = jax.nn.softmax(s, axis=-1).astype(v.dtype)
    o = jnp.einsum('bgrqk,bkgd->bqgrd', p, v)
    return o.reshape(bn, lq, N_HEADS * HEAD_DIM)


def _prompt_band_attention(q, k, v, rel_bias):
    bn, L = q.shape[:2]
    nc = L // CHUNK
    pad = PAST_CHUNKS * CHUNK
    kp = jnp.pad(k, ((0, 0), (pad, 0), (0, 0), (0, 0)))
    vp = jnp.pad(v, ((0, 0), (pad, 0), (0, 0), (0, 0)))
    qc = q.reshape(bn, nc, CHUNK, N_HEADS, HEAD_DIM).transpose(1, 0, 2, 3, 4)

    def one_chunk(args):
        j, qj = args
        start = j * CHUNK
        kj = lax.dynamic_slice_in_dim(kp, start, BAND_LEN, axis=1)
        vj = lax.dynamic_slice_in_dim(vp, start, BAND_LEN, axis=1)
        q_pos = start + jnp.arange(CHUNK)
        k_pos = start - pad + jnp.arange(BAND_LEN)
        return _band_attention(qj, kj, vj, q_pos, k_pos, rel_bias)

    o = lax.map(one_chunk, (jnp.arange(nc), qc))
    return o.transpose(1, 0, 2, 3).reshape(bn, L, N_HEADS * HEAD_DIM)


def _moe(h, w_router, b_router, w_gate, w_up, w_down):
    scores = jax.nn.sigmoid((h @ w_router).astype(jnp.float32))
    sel = scores + b_router.astype(jnp.float32)
    sel_g = sel.reshape(sel.shape[:-1] + (N_EXPERT_GROUPS, EXPERTS_PER_GROUP))
    group_score = jnp.sum(lax.top_k(sel_g, TOP_K)[0], axis=-1)
    g_idx = jnp.argmax(group_score, axis=-1)
    in_group = g_idx[..., None] == jnp.arange(N_EXPERT_GROUPS)
    masked = jnp.where(in_group[..., None], sel_g, NEG_INF).reshape(sel.shape)
    _, e_idx = lax.top_k(masked, TOP_K)
    w_sel = jnp.take_along_axis(scores, e_idx, axis=-1)
    w_sel = w_sel / jnp.sum(w_sel, axis=-1, keepdims=True)
    onehot = (e_idx[..., None] == jnp.arange(N_EXPERTS)).astype(jnp.float32)
    gates = jnp.sum(onehot * w_sel[..., None], axis=-2).astype(h.dtype)
    out = jnp.zeros_like(h)
    for e in range(N_EXPERTS):
        a = jax.nn.silu(h @ w_gate[e]) * (h @ w_up[e])
        out = out + gates[..., e:e + 1] * (a @ w_down[e])
    return out


def _trunk(x, c, attend, P):
    bn, L, _ = x.shape
    h = x
    a_states = []
    k = None
    v = None
    for layer in range(DEPTH):
        sh1, sc1, gt1, sh2, sc2, gt2 = _ada(c, P['w_ada'][layer], P['b_ada'][layer], 6)
        hn = _modulate(_rmsnorm(h, P['g_mix'][layer]), sh1, sc1)
        if layer < N_A_LAYERS:
            i = layer
            out, v_rows = _chunk_mlp_mix(hn, P['a_w_in'][i], P['a_b_in'][i], P['a_ln_g'][i],
                                         P['a_ln_b'][i], P['a_w_s'][i], P['a_b_s'][i],
                                         P['a_w_out'][i])
            a_states.append(v_rows)
        else:
            i = layer - N_A_LAYERS
            q = (hn @ P['b_w_q'][i]).reshape(bn, L, N_HEADS, HEAD_DIM)
            out = attend(q, k, v, P['b_rel_bias'][i]) @ P['b_w_o'][i]
        h = h + gt1 * out
        hn = _modulate(_rmsnorm(h, P['g_ffn'][layer]), sh2, sc2)
        h = h + gt2 * _moe(hn, P['w_router'], P['b_router'], P['w_gate'][layer],
                           P['w_up'][layer], P['w_down'][layer])
        if layer == N_A_LAYERS - 1:
            shk, sck = _ada(c, P['w_ada_kv'], P['b_ada_kv'], 2)
            kvn = _modulate(_rmsnorm(h, P['g_kv']), shk, sck)
            k = (kvn @ P['w_k']).reshape(bn, L, N_KV_HEADS, HEAD_DIM)
            v = (kvn @ P['w_v']).reshape(bn, L, N_KV_HEADS, HEAD_DIM)
    return _rmsnorm(h, P['g_final']), k, v, jnp.stack(a_states)


def setup_inputs(seed: int = 0) -> dict:
    key = jax.random.key(seed)
    ks = jax.random.split(key, 31)
    f32 = jnp.float32
    sD = D_MODEL ** -0.5
    n_cache = min(PAST_CHUNKS * CHUNK, PAST_LEN)

    def nrm(k, shape, scale):
        return jax.random.normal(k, shape, f32) * scale

    return {
        'x_prompt': nrm(ks[0], (BATCH, SEQ, D_MODEL), 1.0),
        'x_sample': nrm(ks[1], (DEC_BATCH, DEC_SEQ, D_MODEL), 1.0),
        'cache_k': nrm(ks[2], (DEC_BATCH, n_cache, N_KV_HEADS, HEAD_DIM), 1.0),
        'cache_v': nrm(ks[3], (DEC_BATCH, n_cache, N_KV_HEADS, HEAD_DIM), 1.0),
        'c_prompt': nrm(ks[4], (BATCH, D_MODEL), 1.0),
        'c_sample': nrm(ks[5], (DEC_BATCH, D_MODEL), 1.0),
        'w_ada': nrm(ks[6], (DEPTH, D_MODEL, 6 * D_MODEL), 0.5 * sD),
        'b_ada': nrm(ks[7], (DEPTH, 6 * D_MODEL), 0.02),
        'g_mix': 1.0 + nrm(ks[8], (DEPTH, D_MODEL), 0.02),
        'g_ffn': 1.0 + nrm(ks[9], (DEPTH, D_MODEL), 0.02),
        'a_w_in': nrm(ks[10], (N_A_LAYERS, D_MODEL, 2 * A_WIDTH), sD),
        'a_b_in': nrm(ks[11], (N_A_LAYERS, 2 * A_WIDTH), 0.02),
        'a_ln_g': 1.0 + nrm(ks[12], (N_A_LAYERS, A_WIDTH), 0.02),
        'a_ln_b': nrm(ks[13], (N_A_LAYERS, A_WIDTH), 0.02),
        'a_w_s': nrm(ks[14], (N_A_LAYERS, A_GROUPS, A_CHUNK, A_CHUNK), A_CHUNK ** -0.5),
        'a_b_s': 1.0 + nrm(ks[15], (N_A_LAYERS, A_GROUPS, A_CHUNK), 0.1),
        'a_w_out': nrm(ks[16], (N_A_LAYERS, A_WIDTH, D_MODEL), A_WIDTH ** -0.5),
        'w_ada_kv': nrm(ks[17], (D_MODEL, 2 * D_MODEL), 0.5 * sD),
        'b_ada_kv': nrm(ks[18], (2 * D_MODEL,), 0.02),
        'g_kv': 1.0 + nrm(ks[19], (D_MODEL,), 0.02),
        'w_k': nrm(ks[20], (D_MODEL, N_KV_HEADS * HEAD_DIM), sD),
        'w_v': nrm(ks[21], (D_MODEL, N_KV_HEADS * HEAD_DIM), sD),
        'b_w_q': nrm(ks[22], (N_B_LAYERS, D_MODEL, N_HEADS * HEAD_DIM), sD),
        'b_rel_bias': nrm(ks[23], (N_B_LAYERS, N_HEADS, N_REL), 0.5),
        'b_w_o': nrm(ks[24], (N_B_LAYERS, N_HEADS * HEAD_DIM, D_MODEL), (N_HEADS * HEAD_DIM) ** -0.5),
        'w_router': nrm(ks[25], (D_MODEL, N_EXPERTS), sD),
        'b_router': nrm(ks[26], (N_EXPERTS,), 0.01),
        'w_gate': nrm(ks[27], (DEPTH, N_EXPERTS, D_MODEL, EXPERT_FF), sD),
        'w_up': nrm(ks[28], (DEPTH, N_EXPERTS, D_MODEL, EXPERT_FF), sD),
        'w_down': nrm(ks[29], (DEPTH, N_EXPERTS, EXPERT_FF, D_MODEL), EXPERT_FF ** -0.5),
        'g_final': 1.0 + nrm(ks[30], (D_MODEL,), 0.02),
    }


def reference(x_prompt, x_sample, cache_k, cache_v, c_prompt, c_sample, w_ada, b_ada,
              g_mix, g_ffn, a_w_in, a_b_in, a_ln_g, a_ln_b, a_w_s, a_b_s, a_w_out,
              w_ada_kv, b_ada_kv, g_kv, w_k, w_v, b_w_q, b_rel_bias, b_w_o,
              w_router, b_router, w_gate, w_up, w_down, g_final):
    P = dict(w_ada=w_ada, b_ada=b_ada, g_mix=g_mix, g_ffn=g_ffn, a_w_in=a_w_in,
             a_b_in=a_b_in, a_ln_g=a_ln_g, a_ln_b=a_ln_b, a_w_s=a_w_s, a_b_s=a_b_s,
             a_w_out=a_w_out, w_ada_kv=w_ada_kv, b_ada_kv=b_ada_kv, g_kv=g_kv, w_k=w_k,
             w_v=w_v, b_w_q=b_w_q, b_rel_bias=b_rel_bias, b_w_o=b_w_o, w_router=w_router,
             b_router=b_router, w_gate=w_gate, w_up=w_up, w_down=w_down, g_final=g_final)

    y_prompt, k_p, v_p, _ = _trunk(x_prompt, c_prompt, _prompt_band_attention, P)
    seq_p = x_prompt.shape[1]
    n_keep = min(PAST_CHUNKS * CHUNK, seq_p)

    def attend_sample(q, k, v, rel_bias):
        n_cache = cache_k.shape[1]
        n_new = q.shape[1]
        k_all = jnp.concatenate([cache_k.astype(k.dtype), k], axis=1)
        v_all = jnp.concatenate([cache_v.astype(v.dtype), v], axis=1)
        k_pos = PAST_LEN - n_cache + jnp.arange(n_cache + n_new)
        q_pos = PAST_LEN + jnp.arange(n_new)
        return _band_attention(q, k_all, v_all, q_pos, k_pos, rel_bias)

    y_sample, k_s, v_s, av_s = _trunk(x_sample, c_sample, attend_sample, P)
    return (y_prompt, y_sample, k_p[:, seq_p - n_keep:], v_p[:, seq_p - n_keep:], k_s, v_s, av_s)
```

```python
import functools

import jax
import jax.numpy as jnp
from jax import lax
from jax.experimental import pallas as pl
from jax.experimental.pallas import tpu as pltpu

F32 = jnp.float32
BF16 = jnp.bfloat16

HEAD_DIM = 128
CHUNK = 64
A_CHUNK = 128
PAST_CHUNKS = 8
PAST_LEN = 4096
MAX_REL = 128
N_EXPERT_GROUPS = 4
EPS = 1e-6
NEG_INF = -1e30

MIB = 1024 * 1024
ROW_TILE = 512
MOE_TILE = 256
PROLOGUE_ROWS = 64
GATHER_ROWS = 512
COMBINE_ROWS = 128


def _cparams(sem, vmem_mib=48):
    return pltpu.CompilerParams(dimension_semantics=sem, vmem_limit_bytes=vmem_mib * MIB)


def _row_tiles(nbatch, seqlen, target):
    if seqlen >= target:
        assert seqlen % target == 0
        return 1, target
    nb = min(nbatch, max(1, target // seqlen))
    assert nbatch % nb == 0
    return nb, seqlen


def _sigmoid(x):
    return 1.0 / (1.0 + jnp.exp(-x))


def _ada_kernel(c_ref, w_ref, b_ref, o_ref):
    c = c_ref[...]
    a = (c * _sigmoid(c)).astype(BF16)
    o_ref[...] = jnp.dot(a, w_ref[...].astype(BF16), preferred_element_type=F32) + b_ref[...]


def _ada(c, w, b, tn=512):
    nl, d, n = w.shape
    bp = c.shape[0]
    tn = min(tn, n)
    return pl.pallas_call(
        _ada_kernel,
        grid=(nl, n // tn),
        in_specs=[pl.BlockSpec((bp, d), lambda l, j: (0, 0)),
                  pl.BlockSpec((None, d, tn), lambda l, j: (l, 0, j)),
                  pl.BlockSpec((None, 1, tn), lambda l, j: (l, 0, j))],
        out_specs=pl.BlockSpec((None, bp, tn), lambda l, j: (l, 0, j)),
        out_shape=jax.ShapeDtypeStruct((nl, bp, n), F32),
        compiler_params=_cparams(("parallel", "parallel")),
        name="ada",
    )(c, w, b.reshape(nl, 1, n))


def _norm_mod_rows(x, g, sh, sc):
    ms = jnp.mean(x * x, axis=-1, keepdims=True)
    y = x * lax.rsqrt(ms + EPS) * g
    return y * (1.0 + sc) + sh


def _norm_mod_to(x_ref, g_ref, sh_ref, sc_ref, dst_ref, nb, lt, dst_dtype):
    rc = min(PROLOGUE_ROWS, lt)
    g = g_ref[...]
    for b in range(nb):
        sh = sh_ref[b:b + 1]
        sc = sc_ref[b:b + 1]

        def body(i, carry, b=b, sh=sh, sc=sc):
            r0 = pl.multiple_of(i * rc, rc)
            y = _norm_mod_rows(x_ref[b:b + 1, pl.ds(r0, rc), :], g, sh, sc)
            dst_ref[pl.ds(b * lt + r0, rc), :] = y[0].astype(dst_dtype)
            return carry

        lax.fori_loop(0, lt // rc, body, 0)


def _norm_mm_kernel(x_ref, g_ref, sh_ref, sc_ref, w_ref, b_ref, o_ref, xs_ref, *, nb, lt, gelu):
    @pl.when(pl.program_id(2) == 0)
    def _():
        _norm_mod_to(x_ref, g_ref, sh_ref, sc_ref, xs_ref, nb, lt, BF16)

    acc = jnp.dot(xs_ref[...], w_ref[...], preferred_element_type=F32) + b_ref[...]
    if gelu:
        acc = jax.nn.gelu(acc, approximate=True)
    o_ref[...] = acc.reshape(nb, lt, acc.shape[-1]).astype(o_ref.dtype)


def _norm_mm(x, g, sh, sc, w, bias, out_dtype, gelu=False, tn=1024):
    bsz, seqlen, d = x.shape
    n = w.shape[1]
    nb, lt = _row_tiles(bsz, seqlen, ROW_TILE)
    tn = min(tn, n)
    kern = functools.partial(_norm_mm_kernel, nb=nb, lt=lt, gelu=gelu)
    return pl.pallas_call(
        kern,
        grid=(bsz // nb, seqlen // lt, n // tn),
        in_specs=[pl.BlockSpec((nb, lt, d), lambda b, l, j: (b, l, 0)),
                  pl.BlockSpec((1, d), lambda b, l, j: (0, 0)),
                  pl.BlockSpec((nb, 1, d), lambda b, l, j: (b, 0, 0)),
                  pl.BlockSpec((nb, 1, d), lambda b, l, j: (b, 0, 0)),
                  pl.BlockSpec((d, tn), lambda b, l, j: (0, j)),
                  pl.BlockSpec((1, tn), lambda b, l, j: (0, j))],
        out_specs=pl.BlockSpec((nb, lt, tn), lambda b, l, j: (b, l, j)),
        out_shape=jax.ShapeDtypeStruct((bsz, seqlen, n), out_dtype),
        scratch_shapes=[pltpu.VMEM((nb * lt, d), BF16)],
        compiler_params=_cparams(("parallel", "parallel", "arbitrary"), 56),
        name="norm_mm",
    )(x, g.reshape(1, d), sh, sc, w, bias.reshape(1, n))


def _mm_res_kernel(x_ref, w_ref, h_ref, gt_ref, o_ref, *, nb, lt):
    x = x_ref[...]
    acc = jnp.dot(x.reshape(nb * lt, x.shape[-1]), w_ref[...], preferred_element_type=F32)
    o_ref[...] = h_ref[...] + gt_ref[...] * acc.reshape(nb, lt, acc.shape[-1])


def _mm_res(x, w, h, gt, tn=1024):
    bsz, seqlen, k = x.shape
    n = w.shape[1]
    nb, lt = _row_tiles(bsz, seqlen, ROW_TILE)
    tn = min(tn, n)
    kern = functools.partial(_mm_res_kernel, nb=nb, lt=lt)
    return pl.pallas_call(
        kern,
        grid=(bsz // nb, seqlen // lt, n // tn),
        in_specs=[pl.BlockSpec((nb, lt, k), lambda b, l, j: (b, l, 0)),
                  pl.BlockSpec((k, tn), lambda b, l, j: (0, j)),
                  pl.BlockSpec((nb, lt, tn), lambda b, l, j: (b, l, j)),
                  pl.BlockSpec((nb, 1, tn), lambda b, l, j: (b, 0, j))],
        out_specs=pl.BlockSpec((nb, lt, tn), lambda b, l, j: (b, l, j)),
        out_shape=jax.ShapeDtypeStruct((bsz, seqlen, n), F32),
        compiler_params=_cparams(("parallel", "parallel", "parallel"), 48),
        name="mm_res",
    )(x, w, h, gt)


def _gate_kernel(u_ref, v_ref, lng_ref, lnb_ref, ws_ref, bst_ref, us_ref, *vln_ref, nb, lt, ct, groups):
    aw = u_ref.shape[-1]
    gd = aw // groups
    row = lax.broadcasted_iota(jnp.int32, (ct, ct), 0)
    col = lax.broadcasted_iota(jnp.int32, (ct, ct), 1)
    causal = col <= row
    wm = [jnp.where(causal, ws_ref[g, :ct, :ct], 0.0).astype(BF16) for g in range(groups)]
    lng = lng_ref[...]
    lnb = lnb_ref[...]
    for b in range(nb):
        for c in range(lt // ct):
            rows = slice(c * ct, (c + 1) * ct)
            v = v_ref[b, rows, :]
            xc = v - jnp.mean(v, axis=-1, keepdims=True)
            vln = xc * lax.rsqrt(jnp.mean(xc * xc, axis=-1, keepdims=True) + EPS) * lng + lnb
            if vln_ref:
                vln_ref[0][b, rows, :] = vln
            vb = vln.astype(BF16)
            for g in range(groups):
                cols = slice(g * gd, (g + 1) * gd)
                s = jnp.dot(wm[g], vb[:, cols], preferred_element_type=F32) + bst_ref[:ct, g:g + 1]
                us_ref[b, rows, cols] = (u_ref[b, rows, cols] * s).astype(BF16)


def _gate(z, ln_g, ln_b, w_s, b_s, want_vln):
    bsz, seqlen, aw2 = z.shape
    aw = aw2 // 2
    groups = w_s.shape[0]
    ct = min(A_CHUNK, seqlen)
    nb, lt = _row_tiles(bsz, seqlen, 2 * A_CHUNK)
    assert lt % ct == 0
    kern = functools.partial(_gate_kernel, nb=nb, lt=lt, ct=ct, groups=groups)
    row_spec = pl.BlockSpec((nb, lt, aw), lambda b, l: (b, l, 0))
    out_shape = [jax.ShapeDtypeStruct((bsz, seqlen, aw), BF16)]
    out_specs = [row_spec]
    if want_vln:
        out_shape.append(jax.ShapeDtypeStruct((bsz, seqlen, aw), F32))
        out_specs.append(row_spec)
    res = pl.pallas_call(
        kern,
        grid=(bsz // nb, seqlen // lt),
        in_specs=[row_spec,
                  pl.BlockSpec((nb, lt, aw), lambda b, l: (b, l, 1)),
                  pl.BlockSpec((1, aw), lambda b, l: (0, 0)),
                  pl.BlockSpec((1, aw), lambda b, l: (0, 0)),
                  pl.BlockSpec(w_s.shape, lambda b, l: (0, 0, 0)),
                  pl.BlockSpec((A_CHUNK, groups), lambda b, l: (0, 0))],
        out_specs=out_specs,
        out_shape=out_shape,
        compiler_params=_cparams(("parallel", "parallel"), 48),
        name="spatial_gate",
    )(z, z, ln_g.reshape(1, aw), ln_b.reshape(1, aw), w_s, b_s.T)
    return (res[0], res[1]) if want_vln else (res[0], None)


def _argmax_first(vals):
    m = vals[0]
    idx = jnp.zeros(m.shape, jnp.int32)
    for i in range(1, len(vals)):
        better = vals[i] > m
        m = jnp.where(better, vals[i], m)
        idx = jnp.where(better, i, idx)
    return m, idx


def _top2(vals):
    m1, i1 = _argmax_first(vals)
    rest = [jnp.where(i1 == i, -jnp.inf, v) for i, v in enumerate(vals)]
    m2, i2 = _argmax_first(rest)
    return m1, i1, m2, i2


def _router_kernel(x_ref, g_ref, sh_ref, sc_ref, wrt_ref, br_ref, xn_ref, idx_ref, gate_ref, *, nb, lt):
    _norm_mod_to(x_ref, g_ref, sh_ref, sc_ref, xn_ref, nb, lt, xn_ref.dtype)
    ne = wrt_ref.shape[0]
    per_group = ne // N_EXPERT_GROUPS
    logits = lax.dot_general(wrt_ref[...], xn_ref[...].astype(BF16), (((1,), (1,)), ((), ())),
                             preferred_element_type=F32)
    scores_all = _sigmoid(logits)
    sel_all = scores_all + br_ref[...]
    scores = [scores_all[e:e + 1] for e in range(ne)]
    sel = [sel_all[e:e + 1] for e in range(ne)]
    group_score = []
    for gi in range(N_EXPERT_GROUPS):
        m1, _, m2, _ = _top2(sel[gi * per_group:(gi + 1) * per_group])
        group_score.append(m1 + m2)
    _, g_idx = _argmax_first(group_score)
    masked = [jnp.where(g_idx == e // per_group, sel[e], NEG_INF) for e in range(ne)]
    _, i1, _, i2 = _top2(masked)
    w1 = sum(jnp.where(i1 == e, scores[e], 0.0) for e in range(ne))
    w2 = sum(jnp.where(i2 == e, scores[e], 0.0) for e in range(ne))
    wsum = w1 + w2
    idx_ref[0:1, :] = i1
    idx_ref[1:2, :] = i2
    gate_ref[0:1, :] = w1 / wsum
    gate_ref[1:2, :] = w2 / wsum


def _router(x, g, sh, sc, w_router_t, b_router):
    bsz, seqlen, d = x.shape
    ne = w_router_t.shape[0]
    nb, lt = _row_tiles(bsz, seqlen, ROW_TILE)
    tm = nb * lt
    nl = seqlen // lt
    kern = functools.partial(_router_kernel, nb=nb, lt=lt)
    tok_spec = pl.BlockSpec((2, tm), lambda b, l: (0, b * nl + l))
    return pl.pallas_call(
        kern,
        grid=(bsz // nb, nl),
        in_specs=[pl.BlockSpec((nb, lt, d), lambda b, l: (b, l, 0)),
                  pl.BlockSpec((1, d), lambda b, l: (0, 0)),
                  pl.BlockSpec((nb, 1, d), lambda b, l: (b, 0, 0)),
                  pl.BlockSpec((nb, 1, d), lambda b, l: (b, 0, 0)),
                  pl.BlockSpec((ne, d), lambda b, l: (0, 0)),
                  pl.BlockSpec((ne, 1), lambda b, l: (0, 0))],
        out_specs=[pl.BlockSpec((tm, d), lambda b, l: (b * nl + l, 0)), tok_spec, tok_spec],
        out_shape=[jax.ShapeDtypeStruct((bsz * seqlen, d), F32),
                   jax.ShapeDtypeStruct((2, bsz * seqlen), jnp.int32),
                   jax.ShapeDtypeStruct((2, bsz * seqlen), F32)],
        compiler_params=_cparams(("parallel", "parallel"), 48),
        name="router",
    )(x, g.reshape(1, d), sh, sc, w_router_t, b_router.reshape(ne, 1))


def _row_gather_kernel(src_ref, x_hbm, o_hbm, sem, *, n_rows, batch):
    n_batches = n_rows // batch

    def issue(c):
        def body(r, carry):
            row = c * batch + r
            pltpu.make_async_copy(x_hbm.at[pl.ds(src_ref[row], 1)], o_hbm.at[pl.ds(row, 1)],
                                  sem.at[c % 2]).start()
            return carry
        lax.fori_loop(0, batch, body, 0)

    def wait(c):
        pltpu.make_async_copy(x_hbm.at[pl.ds(0, batch)], o_hbm.at[pl.ds(c * batch, batch)],
                              sem.at[c % 2]).wait()

    issue(0)

    def loop(c, carry):
        @pl.when(c + 1 < n_batches)
        def _():
            issue(c + 1)
        wait(c)
        return carry

    lax.fori_loop(0, n_batches, loop, 0)


def _row_gather(x, src):
    n_rows = src.shape[0]
    batch = min(GATHER_ROWS, n_rows)
    assert n_rows % batch == 0 and x.shape[0] >= batch
    kern = functools.partial(_row_gather_kernel, n_rows=n_rows, batch=batch)
    return pl.pallas_call(
        kern,
        grid_spec=pltpu.PrefetchScalarGridSpec(
            num_scalar_prefetch=1,
            grid=(1,),
            in_specs=[pl.BlockSpec(memory_space=pl.ANY)],
            out_specs=pl.BlockSpec(memory_space=pl.ANY),
            scratch_shapes=[pltpu.SemaphoreType.DMA((2,))]),
        out_shape=jax.ShapeDtypeStruct((n_rows, x.shape[1]), x.dtype),
        compiler_params=_cparams(("arbitrary",), 16),
        name="row_gather",
    )(src, x)


def _moe_up_kernel(te_ref, nu_ref, x_ref, wg_ref, wu_ref, h_ref):
    @pl.when(pl.program_id(1) < nu_ref[0])
    def _():
        x = x_ref[...].astype(BF16)
        g = jnp.dot(x, wg_ref[...], preferred_element_type=F32)
        u = jnp.dot(x, wu_ref[...], preferred_element_type=F32)
        h_ref[...] = (g * _sigmoid(g) * u).astype(BF16)

    @pl.when(pl.program_id(1) >= nu_ref[0])
    def _():
        h_ref[...] = jnp.zeros(h_ref.shape, h_ref.dtype)


def _moe_up(xs, w_gate, w_up, tile_expert, n_used, tf=512):
    n_rows, d = xs.shape
    ff = w_gate.shape[2]
    tf = min(tf, ff)
    n_tiles = n_rows // MOE_TILE
    w_spec = pl.BlockSpec((None, d, tf), lambda j, t, te, nu: (te[t], 0, j))
    return pl.pallas_call(
        _moe_up_kernel,
        grid_spec=pltpu.PrefetchScalarGridSpec(
            num_scalar_prefetch=2,
            grid=(ff // tf, n_tiles),
            in_specs=[pl.BlockSpec((MOE_TILE, d), lambda j, t, te, nu: (t, 0)), w_spec, w_spec],
            out_specs=pl.BlockSpec((MOE_TILE, tf), lambda j, t, te, nu: (t, j))),
        out_shape=jax.ShapeDtypeStruct((n_rows, ff), BF16),
        compiler_params=_cparams(("parallel", "arbitrary"), 48),
        name="moe_up",
    )(tile_expert, n_used, xs, w_gate, w_up)


def _moe_down_kernel(te_ref, nu_ref, h_ref, wd_ref, rg_ref, y_ref):
    @pl.when(pl.program_id(0) < nu_ref[0])
    def _():
        y_ref[...] = jnp.dot(h_ref[...], wd_ref[...], preferred_element_type=F32) * rg_ref[...]

    @pl.when(pl.program_id(0) >= nu_ref[0])
    def _():
        y_ref[...] = jnp.zeros(y_ref.shape, y_ref.dtype)


def _moe_down(hs, w_down, row_gate, tile_expert, n_used):
    n_rows, ff = hs.shape
    d = w_down.shape[2]
    n_tiles = n_rows // MOE_TILE
    return pl.pallas_call(
        _moe_down_kernel,
        grid_spec=pltpu.PrefetchScalarGridSpec(
            num_scalar_prefetch=2,
            grid=(n_tiles,),
            in_specs=[pl.BlockSpec((MOE_TILE, ff), lambda t, te, nu: (t, 0)),
                      pl.BlockSpec((None, ff, d), lambda t, te, nu: (te[t], 0, 0)),
                      pl.BlockSpec((MOE_TILE, 1), lambda t, te, nu: (t, 0))],
            out_specs=pl.BlockSpec((MOE_TILE, d), lambda t, te, nu: (t, 0))),
        out_shape=jax.ShapeDtypeStruct((n_rows, d), F32),
        compiler_params=_cparams(("arbitrary",), 48),
        name="moe_down",
    )(tile_expert, n_used, hs, w_down, row_gate)


def _combine_kernel(pos_ref, h_ref, gt_ref, gf_ref, y_hbm, o_ref, buf, sem, *, nb, lt, n_tok, n_steps,
                    final_norm):
    tc = nb * lt
    i = pl.program_id(0)

    def issue(step, slot):
        def body(r, carry):
            for k in range(2):
                p = pos_ref[k * n_tok + step * tc + r]
                pltpu.make_async_copy(y_hbm.at[pl.ds(p, 1)], buf.at[slot, pl.ds(k * tc + r, 1)],
                                      sem.at[slot]).start()
            return carry
        lax.fori_loop(0, tc, body, 0)

    @pl.when(i == 0)
    def _():
        issue(0, 0)

    @pl.when(i + 1 < n_steps)
    def _():
        issue(i + 1, (i + 1) % 2)

    slot = i % 2
    pltpu.make_async_copy(y_hbm.at[pl.ds(0, 2 * tc)], buf.at[slot], sem.at[slot]).wait()
    moe = buf[slot, 0:tc, :] + buf[slot, tc:2 * tc, :]
    out = h_ref[...] + gt_ref[...] * moe.reshape(nb, lt, moe.shape[-1])
    if final_norm:
        out = out * lax.rsqrt(jnp.mean(out * out, axis=-1, keepdims=True) + EPS) * gf_ref[...]
    o_ref[...] = out


def _combine(h, gt, y_sorted, pos, g_final, final_norm):
    bsz, seqlen, d = h.shape
    nb, lt = _row_tiles(bsz, seqlen, COMBINE_ROWS)
    tc = nb * lt
    nl = seqlen // lt
    n_steps = (bsz // nb) * nl
    assert y_sorted.shape[0] >= 2 * tc
    kern = functools.partial(_combine_kernel, nb=nb, lt=lt, n_tok=bsz * seqlen, n_steps=n_steps,
                             final_norm=final_norm)
    row_spec = pl.BlockSpec((nb, lt, d), lambda i, pos: (i // nl, i % nl, 0))
    return pl.pallas_call(
        kern,
        grid_spec=pltpu.PrefetchScalarGridSpec(
            num_scalar_prefetch=1,
            grid=(n_steps,),
            in_specs=[row_spec,
                      pl.BlockSpec((nb, 1, d), lambda i, pos: (i // nl, 0, 0)),
                      pl.BlockSpec((1, d), lambda i, pos: (0, 0)),
                      pl.BlockSpec(memory_space=pl.ANY)],
            out_specs=row_spec,
            scratch_shapes=[pltpu.VMEM((2, 2 * tc, d), F32), pltpu.SemaphoreType.DMA((2,))]),
        out_shape=jax.ShapeDtypeStruct((bsz, seqlen, d), F32),
        compiler_params=_cparams(("arbitrary",), 40),
        name="moe_combine",
    )(pos, h, gt, g_final.reshape(1, d), y_sorted)


def _moe_layout(idx, gate, n_experts):
    n_tok = idx.shape[1]
    n_slots = 2 * n_tok
    n_tiles = -(-n_slots // MOE_TILE) + n_experts
    n_rows = n_tiles * MOE_TILE
    e = idx.reshape(n_slots)
    onehot = (e[:, None] == jnp.arange(n_experts, dtype=jnp.int32)[None, :]).astype(jnp.int32)
    csum = jnp.cumsum(onehot, axis=0)
    counts = csum[-1]
    rank = jnp.take_along_axis(csum, e[:, None], axis=1)[:, 0] - 1
    padded = ((counts + MOE_TILE - 1) // MOE_TILE) * MOE_TILE
    ends = jnp.cumsum(padded)
    pos = ((ends - padded)[e] + rank).astype(jnp.int32)
    src = jnp.zeros((n_rows,), jnp.int32).at[pos].set(jnp.arange(n_slots, dtype=jnp.int32) % n_tok)
    row_gate = jnp.zeros((n_rows,), F32).at[pos].set(gate.reshape(n_slots))
    tile_start = jnp.arange(n_tiles, dtype=jnp.int32) * MOE_TILE
    tile_expert = jnp.minimum(jnp.searchsorted(ends, tile_start, side="right"), n_experts - 1)
    n_used = (ends[-1] // MOE_TILE).astype(jnp.int32).reshape(1)
    return pos, src, row_gate.reshape(n_rows, 1), tile_expert.astype(jnp.int32), n_used


def _moe_block(h, g, sh, sc, gt, w_router_t, b_router, w_gate, w_up, w_down, g_final, final_norm):
    xn, idx, gate = _router(h, g, sh, sc, w_router_t, b_router)
    pos, src, row_gate, tile_expert, n_used = _moe_layout(idx, gate, w_gate.shape[0])
    xs = _row_gather(xn, src)
    hs = _moe_up(xs, w_gate, w_up, tile_expert, n_used)
    ys = _moe_down(hs, w_down, row_gate, tile_expert, n_used)
    return _combine(h, gt, ys, pos, g_final, final_norm)


def _attn_kernel(q_ref, k_ref, v_ref, bias_ref, o_ref, *, nq, rep, first_chunk):
    band = (PAST_CHUNKS + 1) * CHUNK
    scale = HEAD_DIM ** -0.5
    jq = pl.program_id(2)
    bias = bias_ref[...].reshape(rep * CHUNK, band)
    for c in range(nq):
        jc = jq * nq + c
        start = pl.multiple_of(jc * CHUNK, CHUNK)
        kw = k_ref[pl.ds(start, band), :]
        vw = v_ref[pl.ds(start, band), :]
        qc = q_ref[c * CHUNK:(c + 1) * CHUNK, :]
        q4 = jnp.concatenate([qc[:, r * HEAD_DIM:(r + 1) * HEAD_DIM] for r in range(rep)], axis=0)
        s = lax.dot_general(q4, kw, (((1,), (1,)), ((), ())), preferred_element_type=F32) * scale + bias
        if first_chunk < PAST_CHUNKS:
            col = lax.broadcasted_iota(jnp.int32, s.shape, 1)
            s = jnp.where(col >= (PAST_CHUNKS - first_chunk - jc) * CHUNK, s, NEG_INF)
        m = jnp.max(s, axis=-1, keepdims=True)
        p = jnp.exp(s - m)
        inv = 1.0 / jnp.sum(p, axis=-1, keepdims=True)
        o = jnp.dot(p.astype(BF16), vw, preferred_element_type=F32) * inv
        for r in range(rep):
            o_ref[c * CHUNK:(c + 1) * CHUNK, r * HEAD_DIM:(r + 1) * HEAD_DIM] = (
                o[r * CHUNK:(r + 1) * CHUNK].astype(o_ref.dtype))


def _band_attention(q, k_pad, v_pad, bias, first_chunk):
    bsz, seqlen, hd = q.shape
    lk = k_pad.shape[1]
    kvh = k_pad.shape[2] // HEAD_DIM
    rep = hd // HEAD_DIM // kvh
    nq = min(8, seqlen // CHUNK)
    band = (PAST_CHUNKS + 1) * CHUNK
    kern = functools.partial(_attn_kernel, nq=nq, rep=rep, first_chunk=first_chunk)
    kv_spec = pl.BlockSpec((None, lk, HEAD_DIM), lambda b, g, j: (b, 0, g))
    q_spec = pl.BlockSpec((None, nq * CHUNK, rep * HEAD_DIM), lambda b, g, j: (b, j, g))
    return pl.pallas_call(
        kern,
        grid=(bsz, kvh, seqlen // (nq * CHUNK)),
        in_specs=[q_spec, kv_spec, kv_spec,
                  pl.BlockSpec((rep, CHUNK, band), lambda b, g, j: (g, 0, 0))],
        out_specs=q_spec,
        out_shape=jax.ShapeDtypeStruct((bsz, seqlen, hd), BF16),
        compiler_params=_cparams(("parallel", "parallel", "parallel"), 48),
        name="band_attention",
    )(q, k_pad, v_pad, bias)


def _bias_tiles(rel_bias):
    band = (PAST_CHUNKS + 1) * CHUNK
    qi = jnp.arange(CHUNK)[:, None]
    ki = jnp.arange(band)[None, :]
    rel = jnp.clip(qi + PAST_CHUNKS * CHUNK - ki, -MAX_REL, MAX_REL) + MAX_REL
    return rel_bias[:, rel].astype(F32)


def _trunk(x, mods, mods_kv, cache, P, first_chunk, want_vln):
    bsz, seqlen, d = x.shape

    def split(m, n):
        return [t[:, None, :] for t in jnp.split(m, n, axis=-1)]

    sh1, sc1, gt1, sh2, sc2, gt2 = split(mods[0], 6)
    z = _norm_mm(x, P["g_mix"][0], sh1, sc1, P["a_w_in"], P["a_b_in"][0], F32, gelu=True)
    us, vln = _gate(z, P["a_ln_g"][0], P["a_ln_b"][0], P["a_w_s"][0], P["a_b_s"][0], want_vln)
    h = _mm_res(us, P["a_w_out"], x, gt1)
    h = _moe_block(h, P["g_ffn"][0], sh2, sc2, gt2, P["w_router_t"], P["b_router"],
                   P["w_gate"][0], P["w_up"][0], P["w_down"][0], P["g_final"], False)

    shk, sck = split(mods_kv, 2)
    nkv = P["w_kv"].shape[1] // 2
    kv = _norm_mm(h, P["g_kv"], shk, sck, P["w_kv"], jnp.zeros((2 * nkv,), F32), F32)
    k_new, v_new = kv[..., :nkv], kv[..., nkv:]
    if cache is None:
        pad = jnp.zeros((bsz, PAST_CHUNKS * CHUNK, nkv), BF16)
        k_pad = jnp.concatenate([pad, k_new.astype(BF16)], axis=1)
        v_pad = jnp.concatenate([pad, v_new.astype(BF16)], axis=1)
    else:
        k_pad = jnp.concatenate([cache[0].reshape(bsz, -1, nkv).astype(BF16), k_new.astype(BF16)], axis=1)
        v_pad = jnp.concatenate([cache[1].reshape(bsz, -1, nkv).astype(BF16), v_new.astype(BF16)], axis=1)

    sh1, sc1, gt1, sh2, sc2, gt2 = split(mods[1], 6)
    q = _norm_mm(h, P["g_mix"][1], sh1, sc1, P["b_w_q"], jnp.zeros((P["b_w_q"].shape[1],), F32), BF16)
    o = _band_attention(q, k_pad, v_pad, P["bias_tiles"], first_chunk)
    h = _mm_res(o, P["b_w_o"], h, gt1)
    y = _moe_block(h, P["g_ffn"][1], sh2, sc2, gt2, P["w_router_t"], P["b_router"],
                   P["w_gate"][1], P["w_up"][1], P["w_down"][1], P["g_final"], True)
    return y, k_new, v_new, vln


def kernel(x_prompt, x_sample, cache_k, cache_v, c_prompt, c_sample, w_ada, b_ada, g_mix, g_ffn, a_w_in,
           a_b_in, a_ln_g, a_ln_b, a_w_s, a_b_s, a_w_out, w_ada_kv, b_ada_kv, g_kv, w_k, w_v, b_w_q,
           b_rel_bias, b_w_o, w_router, b_router, w_gate, w_up, w_down, g_final):
    assert a_w_in.shape[0] == 1 and b_w_q.shape[0] == 1, "one A layer followed by one B layer"
    bp, seq_p, d = x_prompt.shape
    bs, seq_s, _ = x_sample.shape
    kvh = w_k.shape[1] // HEAD_DIM
    n_cache = cache_k.shape[1]
    assert n_cache == PAST_CHUNKS * CHUNK and seq_s == CHUNK and PAST_LEN % CHUNK == 0

    n_c = bp + bs
    c_all = jnp.concatenate([c_prompt, c_sample, jnp.zeros((-n_c % 8, d), F32)], axis=0)
    mods = _ada(c_all, w_ada, b_ada)
    mods_kv = _ada(c_all, w_ada_kv[None], b_ada_kv[None])[0]

    P = dict(
        g_mix=g_mix, g_ffn=g_ffn, g_kv=g_kv, g_final=g_final,
        a_w_in=a_w_in[0].astype(BF16), a_b_in=a_b_in, a_ln_g=a_ln_g, a_ln_b=a_ln_b, a_w_s=a_w_s,
        a_b_s=a_b_s, a_w_out=a_w_out[0].astype(BF16),
        w_kv=jnp.concatenate([w_k, w_v], axis=1).astype(BF16),
        b_w_q=b_w_q[0].astype(BF16), b_w_o=b_w_o[0].astype(BF16),
        bias_tiles=_bias_tiles(b_rel_bias[0]),
        w_router_t=w_router.T.astype(BF16), b_router=b_router,
        w_gate=w_gate.astype(BF16), w_up=w_up.astype(BF16), w_down=w_down.astype(BF16),
    )

    y_p, k_p, v_p, _ = _trunk(x_prompt, mods[:, :bp], mods_kv[:bp], None, P, 0, False)
    y_s, k_s, v_s, av_s = _trunk(x_sample, mods[:, bp:n_c], mods_kv[bp:n_c], (cache_k, cache_v), P,
                                 PAST_LEN // CHUNK, True)

    n_keep = min(PAST_CHUNKS * CHUNK, seq_p)
    heads = (kvh, HEAD_DIM)
    return (y_p, y_s,
            k_p[:, seq_p - n_keep:].reshape(bp, n_keep, *heads),
            v_p[:, seq_p - n_keep:].reshape(bp, n_keep, *heads),
            k_s.reshape(bs, seq_s, *heads), v_s.reshape(bs, seq_s, *heads),
            av_s[None])
```

```python
import functools

import jax
import jax.numpy as jnp
from jax import lax
from jax.experimental import pallas as pl
from jax.experimental.pallas import tpu as pltpu

F32 = jnp.float32
BF16 = jnp.bfloat16

HEAD_DIM = 128
CHUNK = 64
A_CHUNK = 128
PAST_CHUNKS = 8
PAST_LEN = 4096
MAX_REL = 128
N_EXPERT_GROUPS = 4
EPS = 1e-6
NEG_INF = -1e30

MIB = 1024 * 1024
ROW_TILE = 512
MOE_TILE = 256
PROLOGUE_ROWS = 64
COMBINE_ROWS = 128
LOG2E = 1.4426950408889634


def _cparams(sem, vmem_mib=48):
    return pltpu.CompilerParams(dimension_semantics=sem, vmem_limit_bytes=vmem_mib * MIB)


def _row_tiles(nbatch, seqlen, target):
    if seqlen >= target:
        assert seqlen % target == 0
        return 1, target
    nb = min(nbatch, max(1, target // seqlen))
    assert nbatch % nb == 0
    return nb, seqlen


def _sigmoid(x):
    return 1.0 / (1.0 + jnp.exp(-x))


def _hi_lo(w):
    hi = w.astype(BF16)
    return hi, (w - hi.astype(F32)).astype(BF16)


def _xw(x, w):
    if w.dtype == BF16:
        return jnp.dot(x, w, preferred_element_type=F32)
    hi, lo = _hi_lo(w)
    return jnp.dot(x, hi, preferred_element_type=F32) + jnp.dot(x, lo, preferred_element_type=F32)


def _wxt(w, x):
    dims = (((1,), (1,)), ((), ()))
    if w.dtype == BF16:
        return lax.dot_general(w, x, dims, preferred_element_type=F32)
    hi, lo = _hi_lo(w)
    return (lax.dot_general(hi, x, dims, preferred_element_type=F32)
            + lax.dot_general(lo, x, dims, preferred_element_type=F32))


def _ada_kernel(c_ref, w_ref, b_ref, o_ref):
    c = c_ref[...]
    o_ref[...] = _xw((c * _sigmoid(c)).astype(BF16), w_ref[...]) + b_ref[...]


def _ada(c, w, b, tn=512):
    nl, d, n = w.shape
    bp = c.shape[0]
    tn = min(tn, n)
    return pl.pallas_call(
        _ada_kernel,
        grid=(nl, n // tn),
        in_specs=[pl.BlockSpec((bp, d), lambda l, j: (0, 0)),
                  pl.BlockSpec((None, d, tn), lambda l, j: (l, 0, j)),
                  pl.BlockSpec((None, 1, tn), lambda l, j: (l, 0, j))],
        out_specs=pl.BlockSpec((None, bp, tn), lambda l, j: (l, 0, j)),
        out_shape=jax.ShapeDtypeStruct((nl, bp, n), F32),
        compiler_params=_cparams(("parallel", "parallel")),
        name="ada",
    )(c, w, b.reshape(nl, 1, n))


def _norm_mod_rows(x, g, sh, sc):
    ms = jnp.mean(x * x, axis=-1, keepdims=True)
    y = x * lax.rsqrt(ms + EPS) * g
    return y * (1.0 + sc) + sh


def _norm_mod_to(x_ref, g_ref, sh_ref, sc_ref, dst_ref, nb, lt, dst_dtype):
    rc = min(PROLOGUE_ROWS, lt)
    g = g_ref[...]
    for b in range(nb):
        sh = sh_ref[b:b + 1]
        sc = sc_ref[b:b + 1]

        def body(i, carry, b=b, sh=sh, sc=sc):
            r0 = pl.multiple_of(i * rc, rc)
            y = _norm_mod_rows(x_ref[b:b + 1, pl.ds(r0, rc), :], g, sh, sc)
            dst_ref[pl.ds(b * lt + r0, rc), :] = y[0].astype(dst_dtype)
            return carry

        lax.fori_loop(0, lt // rc, body, 0)


def _norm_mm_kernel(x_ref, g_ref, sh_ref, sc_ref, w_ref, b_ref, o_ref, xs_ref, *, nb, lt, gelu, out_scale,
                    pad_blocks):
    live = pl.program_id(1) >= pad_blocks

    @pl.when(jnp.logical_and(live, pl.program_id(2) == 0))
    def _():
        _norm_mod_to(x_ref, g_ref, sh_ref, sc_ref, xs_ref, nb, lt, BF16)

    @pl.when(live)
    def _():
        acc = _xw(xs_ref[...], w_ref[...]) + b_ref[...]
        if gelu:
            acc = jax.nn.gelu(acc, approximate=True)
        if out_scale != 1.0:
            acc = acc * out_scale
        o_ref[...] = acc.reshape(nb, lt, acc.shape[-1]).astype(o_ref.dtype)

    if pad_blocks:
        @pl.when(jnp.logical_not(live))
        def _():
            o_ref[...] = jnp.zeros(o_ref.shape, o_ref.dtype)


def _norm_mm(x, g, sh, sc, w, bias, out_dtype, gelu=False, out_scale=1.0, pad_blocks=0, tn=1024):
    bsz, seqlen, d = x.shape
    n = w.shape[1]
    nb, lt = _row_tiles(bsz, seqlen, ROW_TILE)
    tn = min(tn if w.dtype == BF16 else tn // 2, n)
    assert pad_blocks == 0 or nb == 1
    kern = functools.partial(_norm_mm_kernel, nb=nb, lt=lt, gelu=gelu, out_scale=out_scale,
                             pad_blocks=pad_blocks)

    def x_map(b, l, j):
        return (b, jnp.maximum(l - pad_blocks, 0), 0)

    return pl.pallas_call(
        kern,
        grid=(bsz // nb, seqlen // lt + pad_blocks, n // tn),
        in_specs=[pl.BlockSpec((nb, lt, d), x_map),
                  pl.BlockSpec((1, d), lambda b, l, j: (0, 0)),
                  pl.BlockSpec((nb, 1, d), lambda b, l, j: (b, 0, 0)),
                  pl.BlockSpec((nb, 1, d), lambda b, l, j: (b, 0, 0)),
                  pl.BlockSpec((d, tn), lambda b, l, j: (0, j)),
                  pl.BlockSpec((1, tn), lambda b, l, j: (0, j))],
        out_specs=pl.BlockSpec((nb, lt, tn), lambda b, l, j: (b, l, j)),
        out_shape=jax.ShapeDtypeStruct((bsz, seqlen + pad_blocks * lt, n), out_dtype),
        scratch_shapes=[pltpu.VMEM((nb * lt, d), BF16)],
        compiler_params=_cparams(("parallel", "parallel", "arbitrary"), 56),
        name="norm_mm",
    )(x, g.reshape(1, d), sh, sc, w, bias.reshape(1, n))


def _mm_res_kernel(x_ref, w_ref, h_ref, gt_ref, o_ref, *, nb, lt):
    x = x_ref[...]
    acc = _xw(x.reshape(nb * lt, x.shape[-1]), w_ref[...])
    o_ref[...] = h_ref[...] + gt_ref[...] * acc.reshape(nb, lt, acc.shape[-1])


def _mm_res(x, w, h, gt, tn=1024):
    bsz, seqlen, k = x.shape
    n = w.shape[1]
    nb, lt = _row_tiles(bsz, seqlen, ROW_TILE)
    tn = min(tn if w.dtype == BF16 else tn // 2, n)
    kern = functools.partial(_mm_res_kernel, nb=nb, lt=lt)
    return pl.pallas_call(
        kern,
        grid=(bsz // nb, seqlen // lt, n // tn),
        in_specs=[pl.BlockSpec((nb, lt, k), lambda b, l, j: (b, l, 0)),
                  pl.BlockSpec((k, tn), lambda b, l, j: (0, j)),
                  pl.BlockSpec((nb, lt, tn), lambda b, l, j: (b, l, j)),
                  pl.BlockSpec((nb, 1, tn), lambda b, l, j: (b, 0, j))],
        out_specs=pl.BlockSpec((nb, lt, tn), lambda b, l, j: (b, l, j)),
        out_shape=jax.ShapeDtypeStruct((bsz, seqlen, n), F32),
        compiler_params=_cparams(("parallel", "parallel", "parallel"), 48),
        name="mm_res",
    )(x, w, h, gt)


def _gate_kernel(u_ref, v_ref, lng_ref, lnb_ref, ws_ref, bst_ref, us_ref, *vln_ref, nb, lt, ct, groups,
                 two_pass):
    aw = u_ref.shape[-1]
    gd = aw // groups
    row = lax.broadcasted_iota(jnp.int32, (ct, ct), 0)
    col = lax.broadcasted_iota(jnp.int32, (ct, ct), 1)
    causal = col <= row
    wm = [jnp.where(causal, ws_ref[g, :ct, :ct], 0.0) for g in range(groups)]
    wm = [_hi_lo(w) if two_pass else (w.astype(BF16),) for w in wm]
    lng = lng_ref[...]
    lnb = lnb_ref[...]
    for b in range(nb):
        for c in range(lt // ct):
            rows = slice(c * ct, (c + 1) * ct)
            v = v_ref[b, rows, :]
            xc = v - jnp.mean(v, axis=-1, keepdims=True)
            vln = xc * lax.rsqrt(jnp.mean(xc * xc, axis=-1, keepdims=True) + EPS) * lng + lnb
            if vln_ref:
                vln_ref[0][b, rows, :] = vln
            vb = vln.astype(BF16)
            for g in range(groups):
                cols = slice(g * gd, (g + 1) * gd)
                s = sum(jnp.dot(w, vb[:, cols], preferred_element_type=F32) for w in wm[g])
                us_ref[b, rows, cols] = (u_ref[b, rows, cols] * (s + bst_ref[:ct, g:g + 1])).astype(BF16)


def _gate(z, ln_g, ln_b, w_s, b_s, want_vln, two_pass):
    bsz, seqlen, aw2 = z.shape
    aw = aw2 // 2
    groups = w_s.shape[0]
    ct = min(A_CHUNK, seqlen)
    nb, lt = _row_tiles(bsz, seqlen, 2 * A_CHUNK)
    assert lt % ct == 0
    kern = functools.partial(_gate_kernel, nb=nb, lt=lt, ct=ct, groups=groups, two_pass=two_pass)
    row_spec = pl.BlockSpec((nb, lt, aw), lambda b, l: (b, l, 0))
    out_shape = [jax.ShapeDtypeStruct((bsz, seqlen, aw), BF16)]
    out_specs = [row_spec]
    if want_vln:
        out_shape.append(jax.ShapeDtypeStruct((bsz, seqlen, aw), F32))
        out_specs.append(row_spec)
    res = pl.pallas_call(
        kern,
        grid=(bsz // nb, seqlen // lt),
        in_specs=[row_spec,
                  pl.BlockSpec((nb, lt, aw), lambda b, l: (b, l, 1)),
                  pl.BlockSpec((1, aw), lambda b, l: (0, 0)),
                  pl.BlockSpec((1, aw), lambda b, l: (0, 0)),
                  pl.BlockSpec(w_s.shape, lambda b, l: (0, 0, 0)),
                  pl.BlockSpec((A_CHUNK, groups), lambda b, l: (0, 0))],
        out_specs=out_specs,
        out_shape=out_shape,
        compiler_params=_cparams(("parallel", "parallel"), 48),
        name="spatial_gate",
    )(z, z, ln_g.reshape(1, aw), ln_b.reshape(1, aw), w_s, b_s.T)
    return (res[0], res[1]) if want_vln else (res[0], None)


def _argmax_first(vals):
    m = vals[0]
    idx = jnp.zeros(m.shape, jnp.int32)
    for i in range(1, len(vals)):
        better = vals[i] > m
        m = jnp.where(better, vals[i], m)
        idx = jnp.where(better, i, idx)
    return m, idx


def _top2(vals):
    m1, i1 = _argmax_first(vals)
    rest = [jnp.where(i1 == i, -jnp.inf, v) for i, v in enumerate(vals)]
    m2, i2 = _argmax_first(rest)
    return m1, i1, m2, i2


def _router_kernel(x_ref, g_ref, sh_ref, sc_ref, wrt_ref, br_ref, xn_ref, idx_ref, gate_ref, *, nb, lt):
    _norm_mod_to(x_ref, g_ref, sh_ref, sc_ref, xn_ref, nb, lt, xn_ref.dtype)
    ne = wrt_ref.shape[0]
    per_group = ne // N_EXPERT_GROUPS
    logits = _wxt(wrt_ref[...], xn_ref[...].astype(BF16))
    scores_all = _sigmoid(logits)
    sel_all = scores_all + br_ref[...]
    scores = [scores_all[e:e + 1] for e in range(ne)]
    sel = [sel_all[e:e + 1] for e in range(ne)]
    group_score = []
    for gi in range(N_EXPERT_GROUPS):
        m1, _, m2, _ = _top2(sel[gi * per_group:(gi + 1) * per_group])
        group_score.append(m1 + m2)
    _, g_idx = _argmax_first(group_score)
    masked = [jnp.where(g_idx == e // per_group, sel[e], NEG_INF) for e in range(ne)]
    _, i1, _, i2 = _top2(masked)
    w1 = sum(jnp.where(i1 == e, scores[e], 0.0) for e in range(ne))
    w2 = sum(jnp.where(i2 == e, scores[e], 0.0) for e in range(ne))
    wsum = w1 + w2
    idx_ref[0:1, :] = i1
    idx_ref[1:2, :] = i2
    gate_ref[0:1, :] = w1 / wsum
    gate_ref[1:2, :] = w2 / wsum


def _router(x, g, sh, sc, w_router_t, b_router):
    bsz, seqlen, d = x.shape
    ne = w_router_t.shape[0]
    nb, lt = _row_tiles(bsz, seqlen, ROW_TILE)
    tm = nb * lt
    nl = seqlen // lt
    kern = functools.partial(_router_kernel, nb=nb, lt=lt)
    tok_spec = pl.BlockSpec((2, tm), lambda b, l: (0, b * nl + l))
    return pl.pallas_call(
        kern,
        grid=(bsz // nb, nl),
        in_specs=[pl.BlockSpec((nb, lt, d), lambda b, l: (b, l, 0)),
                  pl.BlockSpec((1, d), lambda b, l: (0, 0)),
                  pl.BlockSpec((nb, 1, d), lambda b, l: (b, 0, 0)),
                  pl.BlockSpec((nb, 1, d), lambda b, l: (b, 0, 0)),
                  pl.BlockSpec((ne, d), lambda b, l: (0, 0)),
                  pl.BlockSpec((ne, 1), lambda b, l: (0, 0))],
        out_specs=[pl.BlockSpec((tm, d), lambda b, l: (b * nl + l, 0)), tok_spec, tok_spec],
        out_shape=[jax.ShapeDtypeStruct((bsz * seqlen, d), F32),
                   jax.ShapeDtypeStruct((2, bsz * seqlen), jnp.int32),
                   jax.ShapeDtypeStruct((2, bsz * seqlen), F32)],
        compiler_params=_cparams(("parallel", "parallel"), 48),
        name="router",
    )(x, g.reshape(1, d), sh, sc, w_router_t, b_router.reshape(ne, 1))


def _moe_up_kernel(src_ref, te_ref, nu_ref, x_hbm, wg_ref, wu_ref, h_ref, xbuf, sem):
    t = pl.program_id(0)
    n_used = nu_ref[0]

    def gather_tile(tile, slot):
        def body(r, carry):
            tok = src_ref[tile * MOE_TILE + r]
            pltpu.make_async_copy(x_hbm.at[pl.ds(tok, 1)], xbuf.at[slot, pl.ds(r, 1)], sem.at[slot]).start()
            return carry
        lax.fori_loop(0, MOE_TILE, body, 0)

    @pl.when(t == 0)
    def _():
        gather_tile(0, 0)

    @pl.when(t + 1 < n_used)
    def _():
        gather_tile(t + 1, (t + 1) % 2)

    @pl.when(t < n_used)
    def _():
        slot = t % 2
        pltpu.make_async_copy(x_hbm.at[pl.ds(0, MOE_TILE)], xbuf.at[slot], sem.at[slot]).wait()
        x = xbuf[slot].astype(BF16)
        g = jnp.dot(x, wg_ref[...], preferred_element_type=F32)
        u = jnp.dot(x, wu_ref[...], preferred_element_type=F32)
        h_ref[...] = (g * _sigmoid(g) * u).astype(BF16)

    @pl.when(t >= n_used)
    def _():
        h_ref[...] = jnp.zeros(h_ref.shape, h_ref.dtype)


def _moe_up(xn, src, w_gate, w_up, tile_expert, n_used):
    n_tok, d = xn.shape
    ff = w_gate.shape[2]
    n_tiles = tile_expert.shape[0]
    assert n_tok >= MOE_TILE
    w_spec = pl.BlockSpec((None, d, ff), lambda t, src, te, nu: (te[t], 0, 0))
    return pl.pallas_call(
        _moe_up_kernel,
        grid_spec=pltpu.PrefetchScalarGridSpec(
            num_scalar_prefetch=3,
            grid=(n_tiles,),
            in_specs=[pl.BlockSpec(memory_space=pl.ANY), w_spec, w_spec],
            out_specs=pl.BlockSpec((MOE_TILE, ff), lambda t, src, te, nu: (t, 0)),
            scratch_shapes=[pltpu.VMEM((2, MOE_TILE, d), xn.dtype), pltpu.SemaphoreType.DMA((2,))]),
        out_shape=jax.ShapeDtypeStruct((n_tiles * MOE_TILE, ff), BF16),
        compiler_params=_cparams(("arbitrary",), 56),
        name="moe_up",
    )(src, tile_expert, n_used, xn, w_gate, w_up)


def _moe_down_kernel(te_ref, nu_ref, h_ref, wd_ref, y_ref):
    @pl.when(pl.program_id(0) < nu_ref[0])
    def _():
        y_ref[...] = jnp.dot(h_ref[...], wd_ref[...], preferred_element_type=F32)

    @pl.when(pl.program_id(0) >= nu_ref[0])
    def _():
        y_ref[...] = jnp.zeros(y_ref.shape, y_ref.dtype)


def _moe_down(hs, w_down, tile_expert, n_used):
    n_rows, ff = hs.shape
    d = w_down.shape[2]
    n_tiles = n_rows // MOE_TILE
    return pl.pallas_call(
        _moe_down_kernel,
        grid_spec=pltpu.PrefetchScalarGridSpec(
            num_scalar_prefetch=2,
            grid=(n_tiles,),
            in_specs=[pl.BlockSpec((MOE_TILE, ff), lambda t, te, nu: (t, 0)),
                      pl.BlockSpec((None, ff, d), lambda t, te, nu: (te[t], 0, 0))],
            out_specs=pl.BlockSpec((MOE_TILE, d), lambda t, te, nu: (t, 0))),
        out_shape=jax.ShapeDtypeStruct((n_rows, d), F32),
        compiler_params=_cparams(("arbitrary",), 48),
        name="moe_down",
    )(tile_expert, n_used, hs, w_down)


def _combine_kernel(pos_ref, h_ref, gt_ref, gf_ref, eg_ref, y_hbm, o_ref, buf, sem, *, nb, lt, n_tok,
                    n_steps, final_norm):
    tc = nb * lt
    i = pl.program_id(0)

    def issue(step, slot):
        def body(r, carry):
            for k in range(2):
                p = pos_ref[k * n_tok + step * tc + r]
                pltpu.make_async_copy(y_hbm.at[pl.ds(p, 1)], buf.at[slot, pl.ds(k * tc + r, 1)],
                                      sem.at[slot]).start()
            return carry
        lax.fori_loop(0, tc, body, 0)

    @pl.when(i == 0)
    def _():
        issue(0, 0)

    @pl.when(i + 1 < n_steps)
    def _():
        issue(i + 1, (i + 1) % 2)

    slot = i % 2
    pltpu.make_async_copy(y_hbm.at[pl.ds(0, 2 * tc)], buf.at[slot], sem.at[slot]).wait()
    moe = eg_ref[0] * buf[slot, 0:tc, :] + eg_ref[1] * buf[slot, tc:2 * tc, :]
    out = h_ref[...] + gt_ref[...] * moe.reshape(nb, lt, moe.shape[-1])
    if final_norm:
        out = out * lax.rsqrt(jnp.mean(out * out, axis=-1, keepdims=True) + EPS) * gf_ref[...]
    o_ref[...] = out


def _combine(h, gt, y_sorted, pos, gate, g_final, final_norm):
    bsz, seqlen, d = h.shape
    nb, lt = _row_tiles(bsz, seqlen, COMBINE_ROWS)
    tc = nb * lt
    nl = seqlen // lt
    n_steps = (bsz // nb) * nl
    assert y_sorted.shape[0] >= 2 * tc
    kern = functools.partial(_combine_kernel, nb=nb, lt=lt, n_tok=bsz * seqlen, n_steps=n_steps,
                             final_norm=final_norm)
    row_spec = pl.BlockSpec((nb, lt, d), lambda i, pos: (i // nl, i % nl, 0))
    return pl.pallas_call(
        kern,
        grid_spec=pltpu.PrefetchScalarGridSpec(
            num_scalar_prefetch=1,
            grid=(n_steps,),
            in_specs=[row_spec,
                      pl.BlockSpec((nb, 1, d), lambda i, pos: (i // nl, 0, 0)),
                      pl.BlockSpec((1, d), lambda i, pos: (0, 0)),
                      pl.BlockSpec((2, tc, 1), lambda i, pos: (0, i, 0)),
                      pl.BlockSpec(memory_space=pl.ANY)],
            out_specs=row_spec,
            scratch_shapes=[pltpu.VMEM((2, 2 * tc, d), F32), pltpu.SemaphoreType.DMA((2,))]),
        out_shape=jax.ShapeDtypeStruct((bsz, seqlen, d), F32),
        compiler_params=_cparams(("arbitrary",), 40),
        name="moe_combine",
    )(pos, h, gt, g_final.reshape(1, d), gate, y_sorted)


def _moe_layout(idx, n_experts):
    n_tok = idx.shape[1]
    n_slots = 2 * n_tok
    n_tiles = -(-n_slots // MOE_TILE) + n_experts
    n_rows = n_tiles * MOE_TILE
    e = idx.reshape(n_slots)
    onehot = (e[:, None] == jnp.arange(n_experts, dtype=jnp.int32)[None, :]).astype(jnp.int32)
    csum = jnp.cumsum(onehot, axis=0)
    counts = csum[-1]
    rank = jnp.take_along_axis(csum, e[:, None], axis=1)[:, 0] - 1
    padded = ((counts + MOE_TILE - 1) // MOE_TILE) * MOE_TILE
    ends = jnp.cumsum(padded)
    pos = ((ends - padded)[e] + rank).astype(jnp.int32)
    src = jnp.zeros((n_rows,), jnp.int32).at[pos].set(jnp.arange(n_slots, dtype=jnp.int32) % n_tok)
    tile_start = jnp.arange(n_tiles, dtype=jnp.int32) * MOE_TILE
    tile_expert = jnp.sum((ends[None, :] <= tile_start[:, None]).astype(jnp.int32), axis=1)
    tile_expert = jnp.minimum(tile_expert, n_experts - 1)
    n_used = (ends[-1] // MOE_TILE).astype(jnp.int32).reshape(1)
    return pos, src, tile_expert, n_used


def _moe_block(h, g, sh, sc, gt, w_router_t, b_router, w_gate, w_up, w_down, g_final, final_norm):
    xn, idx, gate = _router(h, g, sh, sc, w_router_t, b_router)
    pos, src, tile_expert, n_used = _moe_layout(idx, w_gate.shape[0])
    hs = _moe_up(xn, src, w_gate, w_up, tile_expert, n_used)
    ys = _moe_down(hs, w_down, tile_expert, n_used)
    return _combine(h, gt, ys, pos, gate[:, :, None], g_final, final_norm)


def _attn_kernel(q_ref, k_ref, v_ref, bias_ref, o_ref, *, nq, rep, first_chunk):
    band = (PAST_CHUNKS + 1) * CHUNK
    jq = pl.program_id(2)
    n_masked_steps = max(0, -(-(PAST_CHUNKS - first_chunk) // nq))

    def chunks(masked):
        bias = bias_ref[...].reshape(rep * CHUNK, band)
        for c in range(nq):
            jc = jq * nq + c
            start = pl.multiple_of(jc * CHUNK, CHUNK)
            kw = k_ref[pl.ds(start, band), :]
            vw = v_ref[pl.ds(start, band), :]
            qc = q_ref[c * CHUNK:(c + 1) * CHUNK, :]
            q4 = jnp.concatenate([qc[:, r * HEAD_DIM:(r + 1) * HEAD_DIM] for r in range(rep)], axis=0)
            s = lax.dot_general(q4, kw, (((1,), (1,)), ((), ())), preferred_element_type=F32) + bias
            if masked:
                col = lax.broadcasted_iota(jnp.int32, s.shape, 1)
                s = jnp.where(col >= (PAST_CHUNKS - first_chunk - jc) * CHUNK, s, NEG_INF)
            m = jnp.max(s, axis=-1, keepdims=True)
            p = jnp.exp2(s - m)
            inv = 1.0 / jnp.sum(p, axis=-1, keepdims=True)
            o = jnp.dot(p.astype(BF16), vw, preferred_element_type=F32) * inv
            for r in range(rep):
                o_ref[c * CHUNK:(c + 1) * CHUNK, r * HEAD_DIM:(r + 1) * HEAD_DIM] = (
                    o[r * CHUNK:(r + 1) * CHUNK].astype(o_ref.dtype))

    if n_masked_steps > 0:
        pl.when(jq < n_masked_steps)(lambda: chunks(True))
        pl.when(jq >= n_masked_steps)(lambda: chunks(False))
    else:
        chunks(False)


def _band_attention(q, kv_pad, bias, first_chunk):
    bsz, seqlen, hd = q.shape
    lk = kv_pad.shape[1]
    kvh = kv_pad.shape[2] // (2 * HEAD_DIM)
    rep = hd // HEAD_DIM // kvh
    nq = min(8, seqlen // CHUNK)
    band = (PAST_CHUNKS + 1) * CHUNK
    assert lk == seqlen + PAST_CHUNKS * CHUNK
    kern = functools.partial(_attn_kernel, nq=nq, rep=rep, first_chunk=first_chunk)
    q_spec = pl.BlockSpec((None, nq * CHUNK, rep * HEAD_DIM), lambda b, g, j: (b, j, g))
    return pl.pallas_call(
        kern,
        grid=(bsz, kvh, seqlen // (nq * CHUNK)),
        in_specs=[q_spec,
                  pl.BlockSpec((None, lk, HEAD_DIM), lambda b, g, j: (b, 0, g)),
                  pl.BlockSpec((None, lk, HEAD_DIM), lambda b, g, j: (b, 0, kvh + g)),
                  pl.BlockSpec((rep, CHUNK, band), lambda b, g, j: (g, 0, 0))],
        out_specs=q_spec,
        out_shape=jax.ShapeDtypeStruct((bsz, seqlen, hd), BF16),
        compiler_params=_cparams(("parallel", "parallel", "parallel"), 48),
        name="band_attention",
    )(q, kv_pad, kv_pad, bias)


def _bias_tiles(rel_bias):
    band = (PAST_CHUNKS + 1) * CHUNK
    qi = jnp.arange(CHUNK)[:, None]
    ki = jnp.arange(band)[None, :]
    rel = jnp.clip(qi + PAST_CHUNKS * CHUNK - ki, -MAX_REL, MAX_REL) + MAX_REL
    return rel_bias[:, rel].astype(F32) * LOG2E


def _trunk(x, mods, mods_kv, cache, P, first_chunk, want_vln, n_keep, two_pass):
    bsz, seqlen, d = x.shape
    sfx = "_f32" if two_pass else ""

    def split(m, n):
        return [t[:, None, :] for t in jnp.split(m, n, axis=-1)]

    sh1, sc1, gt1, sh2, sc2, gt2 = split(mods[0], 6)
    z = _norm_mm(x, P["g_mix"][0], sh1, sc1, P["a_w_in" + sfx], P["a_b_in"][0], F32, gelu=True)
    us, vln = _gate(z, P["a_ln_g"][0], P["a_ln_b"][0], P["a_w_s"][0], P["a_b_s"][0], want_vln, two_pass)
    h = _mm_res(us, P["a_w_out" + sfx], x, gt1)
    h = _moe_block(h, P["g_ffn"][0], sh2, sc2, gt2, P["w_router_t" + sfx], P["b_router"],
                   P["w_gate"][0], P["w_up"][0], P["w_down"][0], P["g_final"], False)

    shk, sck = split(mods_kv, 2)
    nkv2 = P["w_kv"].shape[1]
    no_bias = jnp.zeros((nkv2,), F32)
    kv_keep = _norm_mm(h[:, seqlen - n_keep:], P["g_kv"], shk, sck, P["w_kv"], no_bias, F32)
    if cache is None:
        pad_blocks = PAST_CHUNKS * CHUNK // ROW_TILE
        assert pad_blocks * ROW_TILE == PAST_CHUNKS * CHUNK and seqlen % ROW_TILE == 0
        kv_pad = _norm_mm(h, P["g_kv"], shk, sck, P["w_kv"], no_bias, BF16, pad_blocks=pad_blocks)
    else:
        assert n_keep == seqlen
        past = jnp.concatenate([cache[0].reshape(bsz, -1, nkv2 // 2), cache[1].reshape(bsz, -1, nkv2 // 2)],
                               axis=-1)
        kv_pad = jnp.concatenate([past.astype(BF16), kv_keep.astype(BF16)], axis=1)

    sh1, sc1, gt1, sh2, sc2, gt2 = split(mods[1], 6)
    q = _norm_mm(h, P["g_mix"][1], sh1, sc1, P["b_w_q"], jnp.zeros((P["b_w_q"].shape[1],), F32), BF16,
                 out_scale=HEAD_DIM ** -0.5 * LOG2E)
    o = _band_attention(q, kv_pad, P["bias_tiles"], first_chunk)
    h = _mm_res(o, P["b_w_o"], h, gt1)
    y = _moe_block(h, P["g_ffn"][1], sh2, sc2, gt2, P["w_router_t"], P["b_router"],
                   P["w_gate"][1], P["w_up"][1], P["w_down"][1], P["g_final"], True)
    return y, kv_keep[..., :nkv2 // 2], kv_keep[..., nkv2 // 2:], vln


def kernel(x_prompt, x_sample, cache_k, cache_v, c_prompt, c_sample, w_ada, b_ada, g_mix, g_ffn, a_w_in,
           a_b_in, a_ln_g, a_ln_b, a_w_s, a_b_s, a_w_out, w_ada_kv, b_ada_kv, g_kv, w_k, w_v, b_w_q,
           b_rel_bias, b_w_o, w_router, b_router, w_gate, w_up, w_down, g_final):
    assert a_w_in.shape[0] == 1 and b_w_q.shape[0] == 1, "one A layer followed by one B layer"
    bp, seq_p, d = x_prompt.shape
    bs, seq_s, _ = x_sample.shape
    kvh = w_k.shape[1] // HEAD_DIM
    n_cache = cache_k.shape[1]
    assert n_cache == PAST_CHUNKS * CHUNK and seq_s == CHUNK and PAST_LEN % CHUNK == 0

    n_c = bp + bs
    c_all = jnp.concatenate([c_prompt, c_sample, jnp.zeros((-n_c % 8, d), F32)], axis=0)
    mods = _ada(c_all, w_ada, b_ada)
    mods_kv = _ada(c_all, w_ada_kv[None], b_ada_kv[None])[0]

    P = dict(
        g_mix=g_mix, g_ffn=g_ffn, g_kv=g_kv, g_final=g_final,
        a_w_in=a_w_in[0].astype(BF16), a_w_in_f32=a_w_in[0], a_b_in=a_b_in, a_ln_g=a_ln_g, a_ln_b=a_ln_b,
        a_w_s=a_w_s, a_b_s=a_b_s, a_w_out=a_w_out[0].astype(BF16), a_w_out_f32=a_w_out[0],
        w_kv=jnp.concatenate([w_k, w_v], axis=1).astype(BF16),
        b_w_q=b_w_q[0].astype(BF16), b_w_o=b_w_o[0].astype(BF16),
        bias_tiles=_bias_tiles(b_rel_bias[0]),
        w_router_t=w_router.T.astype(BF16), w_router_t_f32=w_router.T, b_router=b_router,
        w_gate=w_gate.astype(BF16), w_up=w_up.astype(BF16), w_down=w_down.astype(BF16),
    )

    n_keep = min(PAST_CHUNKS * CHUNK, seq_p)
    y_p, k_p, v_p, _ = _trunk(x_prompt, mods[:, :bp], mods_kv[:bp], None, P, 0, False, n_keep, False)
    y_s, k_s, v_s, av_s = _trunk(x_sample, mods[:, bp:n_c], mods_kv[bp:n_c], (cache_k, cache_v), P,
                                 PAST_LEN // CHUNK, True, seq_s, True)

    heads = (kvh, HEAD_DIM)
    return (y_p, y_s,
            k_p.reshape(bp, n_keep, *heads), v_p.reshape(bp, n_keep, *heads),
            k_s.reshape(bs, seq_s, *heads), v_s.reshape(bs, seq_s, *heads),
            av_s[None])
```

```python
import functools

import jax
import jax.numpy as jnp
from jax import lax
from jax.experimental import pallas as pl
from jax.experimental.pallas import tpu as pltpu

F32 = jnp.float32
BF16 = jnp.bfloat16

HEAD_DIM = 128
CHUNK = 64
A_CHUNK = 128
PAST_CHUNKS = 8
PAST_LEN = 4096
MAX_REL = 128
N_EXPERT_GROUPS = 4
EPS = 1e-6
NEG_INF = -1e30

MIB = 1024 * 1024
ROW_TILE = 512
ROUTER_TILE = 256
MOE_TILE = 256
PROLOGUE_ROWS = 64
COMBINE_ROWS = 128
LOG2E = 1.4426950408889634


def _cparams(sem, vmem_mib=48):
    return pltpu.CompilerParams(dimension_semantics=sem, vmem_limit_bytes=vmem_mib * MIB)


def _row_tiles(nbatch, seqlen, target):
    if seqlen >= target:
        assert seqlen % target == 0
        return 1, target
    nb = min(nbatch, max(1, target // seqlen))
    assert nbatch % nb == 0
    return nb, seqlen


def _sigmoid(x):
    return 1.0 / (1.0 + jnp.exp(-x))


def _hi_lo(a):
    hi = a.astype(BF16)
    return hi, (a - hi.astype(F32)).astype(BF16)


def _mm3(a, b, dims):
    def dot(u, v):
        return lax.dot_general(u, v, dims, preferred_element_type=F32)

    ah, al = _hi_lo(a) if a.dtype == F32 else (a, None)
    bh, bl = _hi_lo(b) if b.dtype == F32 else (b, None)
    low = [dot(al, bh)] if al is not None else []
    low += [dot(ah, bl)] if bl is not None else []
    return dot(ah, bh) + sum(low) if low else dot(ah, bh)


def _xw(x, w):
    return _mm3(x, w, (((1,), (0,)), ((), ())))


def _wxt(w, x):
    return _mm3(w, x, (((1,), (1,)), ((), ())))


def _ada_kernel(c_ref, w_ref, b_ref, o_ref):
    c = c_ref[...]
    o_ref[...] = _xw(c * _sigmoid(c), w_ref[...]) + b_ref[...]


def _ada(c, w, b, tn=512):
    nl, d, n = w.shape
    bp = c.shape[0]
    tn = min(tn, n)
    return pl.pallas_call(
        _ada_kernel,
        grid=(nl, n // tn),
        in_specs=[pl.BlockSpec((bp, d), lambda l, j: (0, 0)),
                  pl.BlockSpec((None, d, tn), lambda l, j: (l, 0, j)),
                  pl.BlockSpec((None, 1, tn), lambda l, j: (l, 0, j))],
        out_specs=pl.BlockSpec((None, bp, tn), lambda l, j: (l, 0, j)),
        out_shape=jax.ShapeDtypeStruct((nl, bp, n), F32),
        compiler_params=_cparams(("parallel", "parallel")),
        name="ada",
    )(c, w, b.reshape(nl, 1, n))


def _norm_mod_rows(x, g, sh, sc):
    ms = jnp.mean(x * x, axis=-1, keepdims=True)
    y = x * lax.rsqrt(ms + EPS) * g
    return y * (1.0 + sc) + sh


def _norm_mod_to(x_ref, g_ref, sh_ref, sc_ref, dst_ref, nb, lt, dst_dtype):
    rc = min(PROLOGUE_ROWS, lt)
    g = g_ref[...]
    for b in range(nb):
        sh = sh_ref[b:b + 1]
        sc = sc_ref[b:b + 1]

        def body(i, carry, b=b, sh=sh, sc=sc):
            r0 = pl.multiple_of(i * rc, rc)
            y = _norm_mod_rows(x_ref[b:b + 1, pl.ds(r0, rc), :], g, sh, sc)
            dst_ref[pl.ds(b * lt + r0, rc), :] = y[0].astype(dst_dtype)
            return carry

        lax.fori_loop(0, lt // rc, body, 0)


def _norm_mm_kernel(x_ref, g_ref, sh_ref, sc_ref, w_ref, b_ref, o_ref, xs_ref, *, nb, lt, gelu, out_scale,
                    pad_blocks):
    live = pl.program_id(1) >= pad_blocks

    @pl.when(jnp.logical_and(live, pl.program_id(2) == 0))
    def _():
        _norm_mod_to(x_ref, g_ref, sh_ref, sc_ref, xs_ref, nb, lt, xs_ref.dtype)

    @pl.when(live)
    def _():
        acc = _xw(xs_ref[...], w_ref[...]) + b_ref[...]
        if gelu:
            acc = jax.nn.gelu(acc, approximate=True)
        if out_scale != 1.0:
            acc = acc * out_scale
        o_ref[...] = acc.reshape(nb, lt, acc.shape[-1]).astype(o_ref.dtype)

    if pad_blocks:
        @pl.when(jnp.logical_not(live))
        def _():
            o_ref[...] = jnp.zeros(o_ref.shape, o_ref.dtype)


def _norm_mm(x, g, sh, sc, w, bias, out_dtype, gelu=False, out_scale=1.0, pad_blocks=0, tn=1024):
    bsz, seqlen, d = x.shape
    n = w.shape[1]
    nb, lt = _row_tiles(bsz, seqlen, ROW_TILE)
    tn = min(tn if w.dtype == BF16 else tn // 4, n)
    assert pad_blocks == 0 or nb == 1
    kern = functools.partial(_norm_mm_kernel, nb=nb, lt=lt, gelu=gelu, out_scale=out_scale,
                             pad_blocks=pad_blocks)

    def x_map(b, l, j):
        return (b, jnp.maximum(l - pad_blocks, 0), 0)

    return pl.pallas_call(
        kern,
        grid=(bsz // nb, seqlen // lt + pad_blocks, n // tn),
        in_specs=[pl.BlockSpec((nb, lt, d), x_map),
                  pl.BlockSpec((1, d), lambda b, l, j: (0, 0)),
                  pl.BlockSpec((nb, 1, d), lambda b, l, j: (b, 0, 0)),
                  pl.BlockSpec((nb, 1, d), lambda b, l, j: (b, 0, 0)),
                  pl.BlockSpec((d, tn), lambda b, l, j: (0, j)),
                  pl.BlockSpec((1, tn), lambda b, l, j: (0, j))],
        out_specs=pl.BlockSpec((nb, lt, tn), lambda b, l, j: (b, l, j)),
        out_shape=jax.ShapeDtypeStruct((bsz, seqlen + pad_blocks * lt, n), out_dtype),
        scratch_shapes=[pltpu.VMEM((nb * lt, d), w.dtype)],
        compiler_params=_cparams(("parallel", "parallel", "arbitrary"), 56),
        name="norm_mm",
    )(x, g.reshape(1, d), sh, sc, w, bias.reshape(1, n))


def _mm_res_kernel(x_ref, w_ref, h_ref, gt_ref, o_ref, *, nb, lt):
    x = x_ref[...]
    acc = _xw(x.reshape(nb * lt, x.shape[-1]), w_ref[...])
    o_ref[...] = h_ref[...] + gt_ref[...] * acc.reshape(nb, lt, acc.shape[-1])


def _mm_res(x, w, h, gt, tn=1024):
    bsz, seqlen, k = x.shape
    n = w.shape[1]
    assert x.dtype == w.dtype
    nb, lt = _row_tiles(bsz, seqlen, ROW_TILE)
    tn = min(tn if w.dtype == BF16 else tn // 4, n)
    kern = functools.partial(_mm_res_kernel, nb=nb, lt=lt)
    return pl.pallas_call(
        kern,
        grid=(bsz // nb, seqlen // lt, n // tn),
        in_specs=[pl.BlockSpec((nb, lt, k), lambda b, l, j: (b, l, 0)),
                  pl.BlockSpec((k, tn), lambda b, l, j: (0, j)),
                  pl.BlockSpec((nb, lt, tn), lambda b, l, j: (b, l, j)),
                  pl.BlockSpec((nb, 1, tn), lambda b, l, j: (b, 0, j))],
        out_specs=pl.BlockSpec((nb, lt, tn), lambda b, l, j: (b, l, j)),
        out_shape=jax.ShapeDtypeStruct((bsz, seqlen, n), F32),
        compiler_params=_cparams(("parallel", "parallel", "parallel"), 48),
        name="mm_res",
    )(x, w, h, gt)


def _gate_kernel(u_ref, v_ref, lng_ref, lnb_ref, ws_ref, bst_ref, us_ref, *vln_ref, nb, lt, ct, groups,
                 w_dtype):
    aw = u_ref.shape[-1]
    gd = aw // groups
    row = lax.broadcasted_iota(jnp.int32, (ct, ct), 0)
    col = lax.broadcasted_iota(jnp.int32, (ct, ct), 1)
    causal = col <= row
    wm = [jnp.where(causal, ws_ref[g, :ct, :ct], 0.0).astype(w_dtype) for g in range(groups)]
    lng = lng_ref[...]
    lnb = lnb_ref[...]
    for b in range(nb):
        for c in range(lt // ct):
            rows = slice(c * ct, (c + 1) * ct)
            v = v_ref[b, rows, :].astype(F32)
            xc = v - jnp.mean(v, axis=-1, keepdims=True)
            vln = xc * lax.rsqrt(jnp.mean(xc * xc, axis=-1, keepdims=True) + EPS) * lng + lnb
            if vln_ref:
                vln_ref[0][b, rows, :] = vln
            vb = vln.astype(w_dtype)
            for g in range(groups):
                cols = slice(g * gd, (g + 1) * gd)
                s = _xw(wm[g], vb[:, cols]) + bst_ref[:ct, g:g + 1]
                us_ref[b, rows, cols] = (u_ref[b, rows, cols].astype(F32) * s).astype(w_dtype)


def _gate(z, ln_g, ln_b, w_s, b_s, want_vln, w_dtype):
    bsz, seqlen, aw2 = z.shape
    aw = aw2 // 2
    groups = w_s.shape[0]
    ct = min(A_CHUNK, seqlen)
    nb, lt = _row_tiles(bsz, seqlen, 2 * A_CHUNK)
    assert lt % ct == 0
    kern = functools.partial(_gate_kernel, nb=nb, lt=lt, ct=ct, groups=groups, w_dtype=w_dtype)
    row_spec = pl.BlockSpec((nb, lt, aw), lambda b, l: (b, l, 0))
    out_shape = [jax.ShapeDtypeStruct((bsz, seqlen, aw), w_dtype)]
    out_specs = [row_spec]
    if want_vln:
        out_shape.append(jax.ShapeDtypeStruct((bsz, seqlen, aw), F32))
        out_specs.append(row_spec)
    res = pl.pallas_call(
        kern,
        grid=(bsz // nb, seqlen // lt),
        in_specs=[row_spec,
                  pl.BlockSpec((nb, lt, aw), lambda b, l: (b, l, 1)),
                  pl.BlockSpec((1, aw), lambda b, l: (0, 0)),
                  pl.BlockSpec((1, aw), lambda b, l: (0, 0)),
                  pl.BlockSpec(w_s.shape, lambda b, l: (0, 0, 0)),
                  pl.BlockSpec((A_CHUNK, groups), lambda b, l: (0, 0))],
        out_specs=out_specs,
        out_shape=out_shape,
        compiler_params=_cparams(("parallel", "parallel"), 48),
        name="spatial_gate",
    )(z, z, ln_g.reshape(1, aw), ln_b.reshape(1, aw), w_s, b_s.T)
    return (res[0], res[1]) if want_vln else (res[0], None)


def _argmax_first(vals):
    m = vals[0]
    idx = jnp.zeros(m.shape, jnp.int32)
    for i in range(1, len(vals)):
        better = vals[i] > m
        m = jnp.where(better, vals[i], m)
        idx = jnp.where(better, i, idx)
    return m, idx


def _top2(vals):
    m1, i1 = _argmax_first(vals)
    rest = [jnp.where(i1 == i, -jnp.inf, v) for i, v in enumerate(vals)]
    m2, i2 = _argmax_first(rest)
    return m1, i1, m2, i2


def _route(logits, br_ref, idx_ref, gate_ref):
    ne = logits.shape[0]
    per_group = ne // N_EXPERT_GROUPS
    scores_all = _sigmoid(logits)
    sel_all = scores_all + br_ref[...]
    scores = [scores_all[e:e + 1] for e in range(ne)]
    sel = [sel_all[e:e + 1] for e in range(ne)]
    group_score = []
    for gi in range(N_EXPERT_GROUPS):
        m1, _, m2, _ = _top2(sel[gi * per_group:(gi + 1) * per_group])
        group_score.append(m1 + m2)
    _, g_idx = _argmax_first(group_score)
    masked = [jnp.where(g_idx == e // per_group, sel[e], NEG_INF) for e in range(ne)]
    _, i1, _, i2 = _top2(masked)
    w1 = sum(jnp.where(i1 == e, scores[e], 0.0) for e in range(ne))
    w2 = sum(jnp.where(i2 == e, scores[e], 0.0) for e in range(ne))
    wsum = w1 + w2
    idx_ref[0:1, :] = i1
    idx_ref[1:2, :] = i2
    gate_ref[0:1, :] = w1 / wsum
    gate_ref[1:2, :] = w2 / wsum


def _router_kernel(*refs, tiles, keep_low_bits):
    nt = len(tiles)
    g_ref, wrt_ref, wrt32_ref, br_ref, xn_ref, idx_ref, gate_ref = refs[3 * nt:]
    i = pl.program_id(0)
    first = 0
    for k, (nb, lt, n_tiles) in enumerate(tiles):
        x_ref, sh_ref, sc_ref = refs[3 * k:3 * k + 3]

        @pl.when(jnp.logical_and(i >= first, i < first + n_tiles))
        def _(x_ref=x_ref, sh_ref=sh_ref, sc_ref=sc_ref, nb=nb, lt=lt, k=k):
            _norm_mod_to(x_ref, g_ref, sh_ref, sc_ref, xn_ref, nb, lt, xn_ref.dtype)
            if keep_low_bits[k]:
                logits = _wxt(wrt32_ref[...], xn_ref[...])
            else:
                logits = _wxt(wrt_ref[...], xn_ref[...].astype(BF16))
            _route(logits, br_ref, idx_ref, gate_ref)

        first += n_tiles


def _router(trunks, g, w_router_t, b_router, keep_low_bits):
    d = trunks[0][0].shape[2]
    ne = w_router_t.shape[0]
    tiles, in_specs, args = [], [], []
    first = 0
    for x, sh, sc in trunks:
        bsz, seqlen, _ = x.shape
        nb, lt = _row_tiles(bsz, seqlen, ROUTER_TILE)
        assert nb * lt == ROUTER_TILE
        nl = seqlen // lt
        n_tiles = (bsz // nb) * nl

        def local(i, first=first, n_tiles=n_tiles):
            return jnp.clip(i - first, 0, n_tiles - 1)

        in_specs += [pl.BlockSpec((nb, lt, d), lambda i, local=local, nl=nl: (local(i) // nl, local(i) % nl, 0)),
                     pl.BlockSpec((nb, 1, d), lambda i, local=local, nl=nl: (local(i) // nl, 0, 0)),
                     pl.BlockSpec((nb, 1, d), lambda i, local=local, nl=nl: (local(i) // nl, 0, 0))]
        args += [x, sh, sc]
        tiles.append((nb, lt, n_tiles))
        first += n_tiles
    n_tok = first * ROUTER_TILE
    kern = functools.partial(_router_kernel, tiles=tuple(tiles), keep_low_bits=tuple(keep_low_bits))
    tok_spec = pl.BlockSpec((2, ROUTER_TILE), lambda i: (0, i))
    return pl.pallas_call(
        kern,
        grid=(first,),
        in_specs=in_specs + [pl.BlockSpec((1, d), lambda i: (0, 0)),
                             pl.BlockSpec((ne, d), lambda i: (0, 0)),
                             pl.BlockSpec((ne, d), lambda i: (0, 0)),
                             pl.BlockSpec((ne, 1), lambda i: (0, 0))],
        out_specs=[pl.BlockSpec((ROUTER_TILE, d), lambda i: (i, 0)), tok_spec, tok_spec],
        out_shape=[jax.ShapeDtypeStruct((n_tok, d), F32),
                   jax.ShapeDtypeStruct((2, n_tok), jnp.int32),
                   jax.ShapeDtypeStruct((2, n_tok), F32)],
        compiler_params=_cparams(("parallel",), 56),
        name="router",
    )(*args, g.reshape(1, d), w_router_t.astype(BF16), w_router_t, b_router.reshape(ne, 1))


def _moe_up_kernel(src_ref, te_ref, nu_ref, x_hbm, wg_ref, wu_ref, h_ref, xbuf, sem):
    t = pl.program_id(0)
    n_used = nu_ref[0]

    def start_row(tile, slot, r):
        tok = src_ref[tile * MOE_TILE + r]
        pltpu.make_async_copy(x_hbm.at[pl.ds(tok, 1)], xbuf.at[slot, pl.ds(r, 1)], sem.at[slot]).start()

    def wait_tile(slot):
        pltpu.make_async_copy(x_hbm.at[pl.ds(0, MOE_TILE)], xbuf.at[slot], sem.at[slot]).wait()

    @pl.when(t == 0)
    def _():
        lax.fori_loop(0, MOE_TILE, lambda r, c: (start_row(0, 0, r), c)[1], 0)

    @pl.when(t < n_used)
    def _():
        wait_tile(t % 2)
        for r in range(MOE_TILE):
            start_row(t + 1, (t + 1) % 2, r)
        x = xbuf[t % 2].astype(BF16)
        g = jnp.dot(x, wg_ref[...], preferred_element_type=F32)
        u = jnp.dot(x, wu_ref[...], preferred_element_type=F32)
        h_ref[...] = (g * _sigmoid(g) * u).astype(BF16)

    @pl.when(t == n_used)
    def _():
        wait_tile(t % 2)

    @pl.when(t >= n_used)
    def _():
        h_ref[...] = jnp.zeros(h_ref.shape, h_ref.dtype)


def _moe_up(xn, src, w_gate, w_up, layer, tile_expert, n_used):
    n_tok, d = xn.shape
    ff = w_gate.shape[3]
    n_tiles = tile_expert.shape[0]
    assert n_tok >= MOE_TILE
    w_spec = pl.BlockSpec((None, None, d, ff), lambda t, src, te, nu: (layer, te[t], 0, 0))
    return pl.pallas_call(
        _moe_up_kernel,
        grid_spec=pltpu.PrefetchScalarGridSpec(
            num_scalar_prefetch=3,
            grid=(n_tiles,),
            in_specs=[pl.BlockSpec(memory_space=pl.ANY), w_spec, w_spec],
            out_specs=pl.BlockSpec((MOE_TILE, ff), lambda t, src, te, nu: (t, 0)),
            scratch_shapes=[pltpu.VMEM((2, MOE_TILE, d), xn.dtype), pltpu.SemaphoreType.DMA((2,))]),
        out_shape=jax.ShapeDtypeStruct((n_tiles * MOE_TILE, ff), BF16),
        compiler_params=_cparams(("arbitrary",), 56),
        name="moe_up",
    )(src, tile_expert, n_used, xn, w_gate, w_up)


def _moe_down_kernel(te_ref, nu_ref, h_ref, wd_ref, y_ref):
    @pl.when(pl.program_id(0) < nu_ref[0])
    def _():
        y_ref[...] = jnp.dot(h_ref[...], wd_ref[...], preferred_element_type=F32)

    @pl.when(pl.program_id(0) >= nu_ref[0])
    def _():
        y_ref[...] = jnp.zeros(y_ref.shape, y_ref.dtype)


def _moe_down(hs, w_down, layer, tile_expert, n_used):
    n_rows, ff = hs.shape
    d = w_down.shape[3]
    n_tiles = n_rows // MOE_TILE
    return pl.pallas_call(
        _moe_down_kernel,
        grid_spec=pltpu.PrefetchScalarGridSpec(
            num_scalar_prefetch=2,
            grid=(n_tiles,),
            in_specs=[pl.BlockSpec((MOE_TILE, ff), lambda t, te, nu: (t, 0)),
                      pl.BlockSpec((None, None, ff, d), lambda t, te, nu: (layer, te[t], 0, 0))],
            out_specs=pl.BlockSpec((MOE_TILE, d), lambda t, te, nu: (t, 0))),
        out_shape=jax.ShapeDtypeStruct((n_rows, d), F32),
        compiler_params=_cparams(("arbitrary",), 48),
        name="moe_down",
    )(tile_expert, n_used, hs, w_down)


def _combine_kernel(pos_ref, h_ref, gt_ref, gf_ref, eg_ref, y_hbm, o_ref, buf, sem, *, nb, lt, n_tok,
                    n_steps, final_norm):
    tc = nb * lt
    i = pl.program_id(0)

    def issue(step, slot):
        def body(r, carry):
            for k in range(2):
                p = pos_ref[k * n_tok + step * tc + r]
                pltpu.make_async_copy(y_hbm.at[pl.ds(p, 1)], buf.at[slot, pl.ds(k * tc + r, 1)],
                                      sem.at[slot]).start()
            return carry
        lax.fori_loop(0, tc, body, 0, unroll=4)

    @pl.when(i == 0)
    def _():
        issue(0, 0)

    @pl.when(i + 1 < n_steps)
    def _():
        issue(i + 1, (i + 1) % 2)

    slot = i % 2
    pltpu.make_async_copy(y_hbm.at[pl.ds(0, 2 * tc)], buf.at[slot], sem.at[slot]).wait()
    moe = eg_ref[0] * buf[slot, 0:tc, :] + eg_ref[1] * buf[slot, tc:2 * tc, :]
    out = h_ref[...] + gt_ref[...] * moe.reshape(nb, lt, moe.shape[-1])
    if final_norm:
        out = out * lax.rsqrt(jnp.mean(out * out, axis=-1, keepdims=True) + EPS) * gf_ref[...]
    o_ref[...] = out


def _combine(h, gt, y_sorted, pos, gate, g_final, final_norm):
    bsz, seqlen, d = h.shape
    nb, lt = _row_tiles(bsz, seqlen, COMBINE_ROWS)
    tc = nb * lt
    nl = seqlen // lt
    n_steps = (bsz // nb) * nl
    assert y_sorted.shape[0] >= 2 * tc
    kern = functools.partial(_combine_kernel, nb=nb, lt=lt, n_tok=bsz * seqlen, n_steps=n_steps,
                             final_norm=final_norm)
    row_spec = pl.BlockSpec((nb, lt, d), lambda i, pos: (i // nl, i % nl, 0))
    return pl.pallas_call(
        kern,
        grid_spec=pltpu.PrefetchScalarGridSpec(
            num_scalar_prefetch=1,
            grid=(n_steps,),
            in_specs=[row_spec,
                      pl.BlockSpec((nb, 1, d), lambda i, pos: (i // nl, 0, 0)),
                      pl.BlockSpec((1, d), lambda i, pos: (0, 0)),
                      pl.BlockSpec((2, tc, 1), lambda i, pos: (0, i, 0)),
                      pl.BlockSpec(memory_space=pl.ANY)],
            out_specs=row_spec,
            scratch_shapes=[pltpu.VMEM((2, 2 * tc, d), F32), pltpu.SemaphoreType.DMA((2,))]),
        out_shape=jax.ShapeDtypeStruct((bsz, seqlen, d), F32),
        compiler_params=_cparams(("arbitrary",), 40),
        name="moe_combine",
    )(pos, h, gt, g_final.reshape(1, d), gate, y_sorted)


def _moe_layout(idx, n_experts):
    n_tok = idx.shape[1]
    n_slots = 2 * n_tok
    n_tiles = -(-n_slots // MOE_TILE) + n_experts
    n_rows = n_tiles * MOE_TILE
    e = idx.reshape(n_slots)
    onehot = (e[:, None] == jnp.arange(n_experts, dtype=jnp.int32)[None, :]).astype(jnp.int32)
    csum = jnp.cumsum(onehot, axis=0)
    counts = csum[-1]
    padded = ((counts + MOE_TILE - 1) // MOE_TILE) * MOE_TILE
    ends = jnp.cumsum(padded)
    pos = jnp.sum(onehot * (csum - 1 + (ends - padded)[None, :]), axis=1).astype(jnp.int32)
    src = jnp.zeros((n_rows,), jnp.int32).at[pos].set(jnp.arange(n_slots, dtype=jnp.int32) % n_tok)
    tile_start = jnp.arange(n_tiles, dtype=jnp.int32) * MOE_TILE
    tile_expert = jnp.sum((ends[None, :] <= tile_start[:, None]).astype(jnp.int32), axis=1)
    tile_expert = jnp.minimum(tile_expert, n_experts - 1)
    n_used = (ends[-1] // MOE_TILE).astype(jnp.int32).reshape(1)
    return pos, src, tile_expert, n_used


def _moe_block(trunks, layer, P, keep_low_bits, final_norm):
    xn, idx, gate = _router([(h, sh, sc) for h, sh, sc, _ in trunks], P["g_ffn"][layer], P["w_router_t"],
                            P["b_router"], keep_low_bits)
    pos, src, tile_expert, n_used = _moe_layout(idx, P["w_gate"].shape[1])
    hs = _moe_up(xn, src, P["w_gate"], P["w_up"], layer, tile_expert, n_used)
    ys = _moe_down(hs, P["w_down"], layer, tile_expert, n_used)
    n_all = idx.shape[1]
    outs, first = [], 0
    for h, _, _, gt in trunks:
        n_tok = h.shape[0] * h.shape[1]
        pos_k = jnp.concatenate([pos[first:first + n_tok], pos[n_all + first:n_all + first + n_tok]])
        outs.append(_combine(h, gt, ys, pos_k, gate[:, first:first + n_tok, None], P["g_final"], final_norm))
        first += n_tok
    return outs


def _attn_kernel(q_ref, k_ref, v_ref, bias_ref, o_ref, *, nq, gs, rep, first_chunk):
    span = (PAST_CHUNKS + gs) * CHUNK
    rows = gs * CHUNK
    jq = pl.program_id(2)
    n_masked_steps = max(0, -(-(PAST_CHUNKS - first_chunk) // nq))
    ones = jnp.ones((span, HEAD_DIM), BF16)

    def passes(masked):
        bias = bias_ref[...].reshape(rep * rows, span)
        for c in range(nq // gs):
            jc = jq * nq + gs * c
            start = pl.multiple_of(jc * CHUNK, CHUNK)
            kw = k_ref[pl.ds(start, span), :]
            vw = jnp.concatenate([v_ref[pl.ds(start, span), :], ones], axis=1)
            qc = q_ref[c * rows:(c + 1) * rows, :]
            q4 = jnp.concatenate([qc[:, r * HEAD_DIM:(r + 1) * HEAD_DIM] for r in range(rep)], axis=0)
            s = lax.dot_general(q4, kw, (((1,), (1,)), ((), ())), preferred_element_type=F32) + bias
            if masked:
                col = lax.broadcasted_iota(jnp.int32, s.shape, 1)
                s = jnp.where(col >= (PAST_CHUNKS - first_chunk - jc) * CHUNK, s, NEG_INF)
            p = jnp.exp2(s - jnp.max(s, axis=-1, keepdims=True))
            ol = jnp.dot(p.astype(BF16), vw, preferred_element_type=F32)
            o = ol[:, :HEAD_DIM] * (1.0 / ol[:, HEAD_DIM:HEAD_DIM + 1])
            for r in range(rep):
                o_ref[c * rows:(c + 1) * rows, r * HEAD_DIM:(r + 1) * HEAD_DIM] = (
                    o[r * rows:(r + 1) * rows].astype(o_ref.dtype))

    if n_masked_steps > 0:
        pl.when(jq < n_masked_steps)(lambda: passes(True))
        pl.when(jq >= n_masked_steps)(lambda: passes(False))
    else:
        passes(False)


def _band_attention(q, kv_pad, rel_bias, first_chunk):
    bsz, seqlen, hd = q.shape
    lk = kv_pad.shape[1]
    kvh = kv_pad.shape[2] // (2 * HEAD_DIM)
    rep = hd // HEAD_DIM // kvh
    n_chunks = seqlen // CHUNK
    nq = min(8, n_chunks)
    gs = 2 if nq % 2 == 0 else 1
    assert lk == seqlen + PAST_CHUNKS * CHUNK and n_chunks % nq == 0
    bias = _bias_tiles(rel_bias, gs)
    kern = functools.partial(_attn_kernel, nq=nq, gs=gs, rep=rep, first_chunk=first_chunk)
    q_spec = pl.BlockSpec((None, nq * CHUNK, rep * HEAD_DIM), lambda b, g, j: (b, j, g))
    return pl.pallas_call(
        kern,
        grid=(bsz, kvh, n_chunks // nq),
        in_specs=[q_spec,
                  pl.BlockSpec((None, lk, HEAD_DIM), lambda b, g, j: (b, 0, g)),
                  pl.BlockSpec((None, lk, HEAD_DIM), lambda b, g, j: (b, 0, kvh + g)),
                  pl.BlockSpec((rep,) + bias.shape[1:], lambda b, g, j: (g, 0, 0))],
        out_specs=q_spec,
        out_shape=jax.ShapeDtypeStruct((bsz, seqlen, hd), BF16),
        compiler_params=_cparams(("parallel", "parallel", "parallel"), 48),
        name="band_attention",
    )(q, kv_pad, kv_pad, bias)


def _bias_tiles(rel_bias, gs):
    nq, nk = gs * CHUNK, (PAST_CHUNKS + gs) * CHUNK
    period = nq + nk - 1
    j = jnp.arange(period)
    diff = jnp.where(j < nk, j, j - period)
    rel = jnp.clip(PAST_CHUNKS * CHUNK - diff, -MAX_REL, MAX_REL) + MAX_REL
    line = rel_bias[:, rel].astype(F32) * LOG2E
    sheared = jnp.tile(line, (1, nq))[:, :nq * (period - 1)].reshape(-1, nq, period - 1)[:, :, :nk]
    qi = jnp.arange(nq)[:, None]
    ki = jnp.arange(nk)[None, :]
    cq, ck = qi // CHUNK, ki // CHUNK
    visible = (ck >= cq) & (ck <= cq + PAST_CHUNKS)
    return jnp.where(visible[None], sheared, NEG_INF)


def _split_mods(m, n):
    return [t[:, None, :] for t in jnp.split(m, n, axis=-1)]


def _mixer_a(x, mods0, P, want_vln, keep_low_bits):
    sfx, wdt = ("_f32", F32) if keep_low_bits else ("", BF16)
    sh1, sc1, gt1 = _split_mods(mods0, 6)[:3]
    z = _norm_mm(x, P["g_mix"][0], sh1, sc1, P["a_w_in" + sfx], P["a_b_in"][0], wdt, gelu=True)
    us, vln = _gate(z, P["a_ln_g"][0], P["a_ln_b"][0], P["a_w_s"][0], P["a_b_s"][0], want_vln, wdt)
    return _mm_res(us, P["a_w_out" + sfx], x, gt1), vln


def _mixer_b(h, mods1, mods_kv, cache, P, first_chunk, n_keep):
    bsz, seqlen, d = h.shape
    shk, sck = _split_mods(mods_kv, 2)
    nkv2 = P["w_kv"].shape[1]
    no_bias = jnp.zeros((nkv2,), F32)
    kv_keep = _norm_mm(h[:, seqlen - n_keep:], P["g_kv"], shk, sck, P["w_kv"], no_bias, F32)
    if cache is None:
        pad_blocks = PAST_CHUNKS * CHUNK // ROW_TILE
        assert pad_blocks * ROW_TILE == PAST_CHUNKS * CHUNK and seqlen % ROW_TILE == 0
        kv_pad = _norm_mm(h, P["g_kv"], shk, sck, P["w_kv"], no_bias, BF16, pad_blocks=pad_blocks)
    else:
        assert n_keep == seqlen
        past = jnp.concatenate([cache[0].reshape(bsz, -1, nkv2 // 2), cache[1].reshape(bsz, -1, nkv2 // 2)],
                               axis=-1)
        kv_pad = jnp.concatenate([past.astype(BF16), kv_keep.astype(BF16)], axis=1)
    sh1, sc1, gt1 = _split_mods(mods1, 6)[:3]
    q = _norm_mm(h, P["g_mix"][1], sh1, sc1, P["b_w_q"], jnp.zeros((P["b_w_q"].shape[1],), F32), BF16,
                 out_scale=HEAD_DIM ** -0.5 * LOG2E)
    o = _band_attention(q, kv_pad, P["b_rel_bias"], first_chunk)
    return _mm_res(o, P["b_w_o"], h, gt1), kv_keep


def _moe_args(hs, mods_l):
    return [(h,) + tuple(_split_mods(m, 6)[3:]) for h, m in zip(hs, mods_l)]


def kernel(x_prompt, x_sample, cache_k, cache_v, c_prompt, c_sample, w_ada, b_ada, g_mix, g_ffn, a_w_in,
           a_b_in, a_ln_g, a_ln_b, a_w_s, a_b_s, a_w_out, w_ada_kv, b_ada_kv, g_kv, w_k, w_v, b_w_q,
           b_rel_bias, b_w_o, w_router, b_router, w_gate, w_up, w_down, g_final):
    assert a_w_in.shape[0] == 1 and b_w_q.shape[0] == 1, "one A layer followed by one B layer"
    bp, seq_p, d = x_prompt.shape
    bs, seq_s, _ = x_sample.shape
    kvh = w_k.shape[1] // HEAD_DIM
    n_cache = cache_k.shape[1]
    assert n_cache == PAST_CHUNKS * CHUNK and seq_s == CHUNK and PAST_LEN % CHUNK == 0

    n_c = bp + bs
    c_all = jnp.concatenate([c_prompt, c_sample, jnp.zeros((-n_c % 8, d), F32)], axis=0)
    mods = _ada(c_all, w_ada, b_ada)
    mods_kv = _ada(c_all, w_ada_kv[None], b_ada_kv[None])[0]

    P = dict(
        g_mix=g_mix, g_ffn=g_ffn, g_kv=g_kv, g_final=g_final,
        a_w_in=a_w_in[0].astype(BF16), a_w_in_f32=a_w_in[0], a_b_in=a_b_in, a_ln_g=a_ln_g, a_ln_b=a_ln_b,
        a_w_s=a_w_s, a_b_s=a_b_s, a_w_out=a_w_out[0].astype(BF16), a_w_out_f32=a_w_out[0],
        w_kv=jnp.concatenate([w_k, w_v], axis=1).astype(BF16),
        b_w_q=b_w_q[0].astype(BF16), b_w_o=b_w_o[0].astype(BF16),
        b_rel_bias=b_rel_bias[0],
        w_router_t=w_router.T, b_router=b_router,
        w_gate=w_gate.astype(BF16), w_up=w_up.astype(BF16), w_down=w_down.astype(BF16),
    )

    mods_p, mods_s = mods[:, :bp], mods[:, bp:n_c]
    low_bits = (False, True)
    h_p, _ = _mixer_a(x_prompt, mods_p[0], P, False, low_bits[0])
    h_s, av_s = _mixer_a(x_sample, mods_s[0], P, True, low_bits[1])
    h_p, h_s = _moe_block(_moe_args([h_p, h_s], [mods_p[0], mods_s[0]]), 0, P, low_bits, False)

    n_keep = min(PAST_CHUNKS * CHUNK, seq_p)
    h_p, kv_p = _mixer_b(h_p, mods_p[1], mods_kv[:bp], None, P, 0, n_keep)
    h_s, kv_s = _mixer_b(h_s, mods_s[1], mods_kv[bp:n_c], (cache_k, cache_v), P, PAST_LEN // CHUNK, seq_s)
    y_p, y_s = _moe_block(_moe_args([h_p, h_s], [mods_p[1], mods_s[1]]), 1, P, (False, False), True)

    nkv = kvh * HEAD_DIM
    heads = (kvh, HEAD_DIM)
    return (y_p, y_s,
            kv_p[..., :nkv].reshape(bp, n_keep, *heads), kv_p[..., nkv:].reshape(bp, n_keep, *heads),
            kv_s[..., :nkv].reshape(bs, seq_s, *heads), kv_s[..., nkv:].reshape(bs, seq_s, *heads),
            av_s[None])
```
